```python
import jax, jax.numpy as jnp
from jax import lax
import numpy as np

D_MODEL = 2048
BATCH = 16
SEQ = 256
DEPTH = 4
DEC_BATCH = 4
DEC_SEQ = 4096
PAST_LEN = 512

GRID_W = 64
N_MIXERS = 2
N_A_LAYERS = (DEPTH + 1) // 2
N_B_LAYERS = DEPTH // 2
DN_HEAD_K = 128
DN_HEAD_V = 128
DN_QK_HEADS = D_MODEL // 128
DN_V_HEADS = 2 * DN_QK_HEADS
DN_KEY_DIM = DN_QK_HEADS * DN_HEAD_K
DN_VAL_DIM = DN_V_HEADS * DN_HEAD_V
DN_CONV_W = 5
DN_CHUNK = 64
DN_CONV_CH = 2 * DN_KEY_DIM + DN_VAL_DIM
DN_IN_DIM = DN_CONV_CH + DN_VAL_DIM + 4 * DN_V_HEADS
CF_WIDTH = 2 * D_MODEL
CF_CONV_W = 31
CF_IN_DIM = 3 * CF_WIDTH
EPS = 1e-6

kernel_name = 'hybrid_deltanet_conformer_diffusion_step'


def rmsnorm(x, w):
    xf = x.astype(jnp.float32)
    y = xf * lax.rsqrt(jnp.mean(xf * xf, axis=-1, keepdims=True) + EPS)
    return (y * w.astype(jnp.float32)).astype(x.dtype)


def layernorm(x, w, b):
    xf = x.astype(jnp.float32)
    mu = jnp.mean(xf, axis=-1, keepdims=True)
    var = jnp.mean(jnp.square(xf - mu), axis=-1, keepdims=True)
    y = (xf - mu) * lax.rsqrt(var + EPS)
    return (y * w.astype(jnp.float32) + b.astype(jnp.float32)).astype(x.dtype)


def l2norm(x):
    xf = x.astype(jnp.float32)
    return xf * lax.rsqrt(jnp.sum(xf * xf, axis=-1, keepdims=True) + EPS)


def adaln(cond, w, b):
    m = jnp.einsum('bd,de->be', jax.nn.silu(cond), w) + b
    shift, scale, gate = jnp.split(m, 3, axis=-1)
    return shift[:, None], scale[:, None], gate[:, None]


def modulated_norm(x, w, shift, scale):
    return rmsnorm(x, w) * (1 + scale) + shift


def dwconv1d(x, w, b):
    K, C = w.shape
    pad = (K - 1) // 2
    y = lax.conv_general_dilated(x, w[:, None, :].astype(x.dtype), window_strides=(1,),
                                 padding=[(pad, pad)], dimension_numbers=('NWC', 'WIO', 'NWC'),
                                 feature_group_count=C)
    return y + b.astype(x.dtype)


def to_col_major(x):
    B, L, C = x.shape
    rows = L // GRID_W
    return x.reshape(B, rows, GRID_W, C).transpose(0, 2, 1, 3).reshape(B, L, C)


def to_row_major(x):
    B, L, C = x.shape
    rows = L // GRID_W
    return x.reshape(B, GRID_W, rows, C).transpose(0, 2, 1, 3).reshape(B, L, C)


def gated_delta_chunked(q, k, v, g, beta, s0):
    Bn, L, H, DK = q.shape
    DV = v.shape[-1]
    C = DN_CHUNK
    N = L // C
    f32 = jnp.float32

    def chunks(t):
        t = t.astype(f32).reshape((Bn, N, C, H) + t.shape[3:])
        return jnp.moveaxis(t, 3, 1)

    q = chunks(q) * (DK ** -0.5)
    k = chunks(k)
    v = chunks(v)
    beta = chunks(beta)
    g = lax.cumsum(chunks(g), axis=3)
    causal = jnp.tril(jnp.ones((C, C), dtype=bool))
    strict = jnp.tril(jnp.ones((C, C), dtype=bool), -1)
    decay = jnp.exp(jnp.where(causal, g[..., :, None] - g[..., None, :], -jnp.inf))
    kb = k * beta[..., None]
    a = jnp.where(strict, jnp.einsum('bhncd,bhnsd->bhncs', kb, k) * decay, 0.0)
    eye = jnp.eye(C, dtype=f32)
    t_inv = lax.linalg.triangular_solve(a + eye, jnp.broadcast_to(eye, a.shape), left_side=True,
                                        lower=True, unit_diagonal=True)
    u = jnp.einsum('bhncs,bhnsv->bhncv', t_inv, v * beta[..., None])
    w = jnp.einsum('bhncs,bhnsd->bhncd', t_inv, kb * jnp.exp(g)[..., None])
    attn = jnp.einsum('bhncd,bhnsd->bhncs', q, k) * decay
    qg = q * jnp.exp(g)[..., None]
    g_last = g[..., -1]
    kd = k * jnp.exp(g_last[..., None] - g)[..., None]

    def step(s, xs):
        u_n, w_n, attn_n, qg_n, kd_n, gl_n = xs
        v_new = u_n - jnp.einsum('bhcd,bhdv->bhcv', w_n, s)
        o = jnp.einsum('bhcd,bhdv->bhcv', qg_n, s) + jnp.einsum('bhcs,bhsv->bhcv', attn_n, v_new)
        s = s * jnp.exp(gl_n)[..., None, None] + jnp.einsum('bhcd,bhcv->bhdv', kd_n, v_new)
        return s, o

    xs = (jnp.moveaxis(u, 2, 0), jnp.moveaxis(w, 2, 0), jnp.moveaxis(attn, 2, 0),
          jnp.moveaxis(qg, 2, 0), jnp.moveaxis(kd, 2, 0), jnp.moveaxis(g_last, 2, 0))
    s_fin, o = lax.scan(step, s0.astype(f32), xs)
    o = jnp.transpose(o, (1, 0, 3, 2, 4)).reshape(Bn, L, H, DV)
    return o, s_fin


def deltanet_branch(h, s0, w_in, conv_w, conv_b, a_log, dt_bias, norm_w, w_out):
    B, L, _ = h.shape
    proj = jnp.einsum('bld,de->ble', h, w_in)
    qkv, z, ab = jnp.split(proj, [DN_CONV_CH, DN_CONV_CH + DN_VAL_DIM], axis=-1)
    qkv = jax.nn.silu(dwconv1d(qkv, conv_w, conv_b))
    q, k, v = jnp.split(qkv, [DN_KEY_DIM, 2 * DN_KEY_DIM], axis=-1)
    rep = DN_V_HEADS // DN_QK_HEADS
    q = jnp.repeat(l2norm(q.reshape(B, L, DN_QK_HEADS, DN_HEAD_K)), rep, axis=2)
    k = jnp.repeat(l2norm(k.reshape(B, L, DN_QK_HEADS, DN_HEAD_K)), rep, axis=2)
    v = v.reshape(B, L, DN_V_HEADS, DN_HEAD_V)
    ab = ab.astype(jnp.float32).reshape(B, L, 2, 2, DN_V_HEADS)
    beta = jax.nn.sigmoid(ab[:, :, :, 0])
    g = -jnp.exp(a_log.astype(jnp.float32)) * jax.nn.softplus(ab[:, :, :, 1] + dt_bias.astype(jnp.float32))
    if s0 is None:
        s0 = jnp.zeros((B, 2, DN_V_HEADS, DN_HEAD_K, DN_HEAD_V), jnp.float32)
    o_f, s_f = gated_delta_chunked(q, k, v, g[:, :, 0], beta[:, :, 0], s0[:, 0])
    o_b, s_b = gated_delta_chunked(jnp.flip(q, 1), jnp.flip(k, 1), jnp.flip(v, 1),
                                   jnp.flip(g[:, :, 1], 1), jnp.flip(beta[:, :, 1], 1), s0[:, 1])
    o = o_f + jnp.flip(o_b, 1)
    o = o * lax.rsqrt(jnp.mean(o * o, axis=-1, keepdims=True) + EPS) * norm_w.astype(jnp.float32)
    o = o * jax.nn.silu(z.astype(jnp.float32).reshape(B, L, DN_V_HEADS, DN_HEAD_V))
    y = jnp.einsum('ble,ed->bld', o.reshape(B, L, DN_VAL_DIM).astype(h.dtype), w_out)
    return y, jnp.stack([s_f, s_b], axis=1)


def conformer_branch(h, w_in, conv_w, conv_b, ln_w, ln_b, w_out, grid_axis):
    B, L, _ = h.shape
    a, b, z = jnp.split(jnp.einsum('bld,de->ble', h, w_in), 3, axis=-1)
    u = a * jax.nn.sigmoid(b)
    E = u.shape[-1]
    if grid_axis is None:
        u = dwconv1d(u, conv_w, conv_b)
    elif grid_axis == 0:
        u = dwconv1d(u.reshape(B * (L // GRID_W), GRID_W, E), conv_w, conv_b).reshape(B, L, E)
    else:
        u = dwconv1d(to_col_major(u).reshape(B * GRID_W, L // GRID_W, E), conv_w, conv_b)
        u = to_row_major(u.reshape(B, L, E))
    u = jax.nn.silu(layernorm(u, ln_w, ln_b))
    return jnp.einsum('ble,ed->bld', u * jax.nn.silu(z), w_out)


def setup_inputs(seed: int = 0) -> dict:
    key = jax.random.key(seed)
    ks = jax.random.split(key, 24)
    f32 = jnp.float32

    def nrm(k, shape, s):
        return jax.random.normal(k, shape, f32) * s

    state_shape = (DEC_BATCH, 2, DN_V_HEADS, DN_HEAD_K, DN_HEAD_V)
    dt = jax.random.uniform(ks[9], (N_A_LAYERS, 2, DN_V_HEADS), f32, 1e-3, 1e-1)
    return {
        'x_prompt': nrm(ks[0], (BATCH, SEQ, D_MODEL), 1.0),
        'x_sample': nrm(ks[1], (DEC_BATCH, DEC_SEQ, D_MODEL), 1.0),
        'state_l0': nrm(ks[2], state_shape, 0.1),
        'state_l2': nrm(ks[3], state_shape, 0.1),
        'c': nrm(ks[4], (DEC_BATCH, D_MODEL), 1.0),
        'c_ctx': nrm(ks[5], (D_MODEL,), 1.0),
        'ada_w': nrm(ks[6], (DEPTH, D_MODEL, 3 * D_MODEL), 0.5 * D_MODEL ** -0.5),
        'ada_b': nrm(ks[7], (DEPTH, 3 * D_MODEL), 0.02),
        'norm_w': 1.0 + nrm(ks[8], (DEPTH, D_MODEL), 0.02),
        'dn_w_in': nrm(ks[10], (N_A_LAYERS, D_MODEL, DN_IN_DIM), D_MODEL ** -0.5),
        'dn_conv_w': nrm(ks[11], (N_A_LAYERS, DN_CONV_W, DN_CONV_CH), DN_CONV_W ** -0.5),
        'dn_conv_b': nrm(ks[12], (N_A_LAYERS, DN_CONV_CH), 0.02),
        'dn_a_log': jnp.log(jax.random.uniform(ks[13], (N_A_LAYERS, 2, DN_V_HEADS), f32, 1.0, 16.0)),
        'dn_dt_bias': dt + jnp.log(-jnp.expm1(-dt)),
        'dn_norm_w': 1.0 + nrm(ks[14], (N_A_LAYERS, DN_HEAD_V), 0.02),
        'dn_w_out': nrm(ks[15], (N_A_LAYERS, DN_VAL_DIM, D_MODEL), DN_VAL_DIM ** -0.5),
        'cf_w_in': nrm(ks[16], (N_B_LAYERS, D_MODEL, CF_IN_DIM), D_MODEL ** -0.5),
        'cf_conv_w': nrm(ks[17], (N_B_LAYERS, CF_CONV_W, CF_WIDTH), CF_CONV_W ** -0.5),
        'cf_conv_b': nrm(ks[18], (N_B_LAYERS, CF_WIDTH), 0.02),
        'cf_ln_w': 1.0 + nrm(ks[19], (N_B_LAYERS, CF_WIDTH), 0.02),
        'cf_ln_b': nrm(ks[20], (N_B_LAYERS, CF_WIDTH), 0.02),
        'cf_w_out': nrm(ks[21], (N_B_LAYERS, CF_WIDTH, D_MODEL), CF_WIDTH ** -0.5),
        'final_norm_w': 1.0 + nrm(ks[22], (D_MODEL,), 0.02),
    }


def reference(x_prompt, x_sample, state_l0, state_l2, c, c_ctx, ada_w, ada_b, norm_w,
              dn_w_in, dn_conv_w, dn_conv_b, dn_a_log, dn_dt_bias, dn_norm_w, dn_w_out,
              cf_w_in, cf_conv_w, cf_conv_b, cf_ln_w, cf_ln_b, cf_w_out, final_norm_w):
    caches = (state_l0, state_l2)
    new_states = []
    xp, xs = x_prompt, x_sample
    for i in range(DEPTH):
        j = i // N_MIXERS
        sh_p, sc_p, gt_p = adaln(c_ctx[None, :], ada_w[i], ada_b[i])
        sh_s, sc_s, gt_s = adaln(c, ada_w[i], ada_b[i])
        hp = modulated_norm(xp, norm_w[i], sh_p, sc_p)
        hs = modulated_norm(xs, norm_w[i], sh_s, sc_s)
        if i % N_MIXERS == 0:
            dn = (dn_w_in[j], dn_conv_w[j], dn_conv_b[j], dn_a_log[j], dn_dt_bias[j], dn_norm_w[j], dn_w_out[j])
            yp, sp = deltanet_branch(hp, None, *dn)
            new_states.append(sp.astype(x_prompt.dtype))
            if j % 2 == 1:
                ys, _ = deltanet_branch(to_col_major(hs), caches[j], *dn)
                ys = to_row_major(ys)
            else:
                ys, _ = deltanet_branch(hs, caches[j], *dn)
        else:
            cf = (cf_w_in[j], cf_conv_w[j], cf_conv_b[j], cf_ln_w[j], cf_ln_b[j], cf_w_out[j])
            yp = conformer_branch(hp, *cf, grid_axis=None)
            ys = conformer_branch(hs, *cf, grid_axis=j % 2)
        xp = xp + gt_p * yp
        xs = xs + gt_s * ys
    y_prompt = rmsnorm(xp, final_norm_w)
    y_sample = rmsnorm(xs, final_norm_w)
    return (y_prompt, y_sample, new_states[0], new_states[1])
```

```python
import functools
import math

import jax
import jax.numpy as jnp
from jax import lax
from jax.experimental import pallas as pl
from jax.experimental.pallas import tpu as pltpu

F32 = jnp.float32
BF16 = jnp.bfloat16
EPS = 1e-6
HI = lax.Precision.HIGHEST

LANES = 128
SUBLANES = 8
VMEM_LIMIT_BYTES = 48 * 1024 * 1024

GRID_W = 64
HEAD = 128
CHUNK = 64
DN_CONV_W = 5
CF_CONV_W = 31
NT_DIMS = (((1,), (1,)), ((), ()))
TN_DIMS = (((0,), (0,)), ((), ()))


def _cparams(sem):
    return pltpu.CompilerParams(dimension_semantics=sem, vmem_limit_bytes=VMEM_LIMIT_BYTES)


def _silu(x):
    return x * jax.nn.sigmoid(x)


def _split_bf16(x):
    hi = x.astype(BF16)
    lo = (x - hi.astype(F32)).astype(BF16)
    return hi, lo


def _ada_kernel(c_ref, w_ref, b_ref, o_ref):
    c = c_ref[...]
    s_hi, s_lo = _split_bf16(_silu(c))
    w_hi, w_lo = _split_bf16(w_ref[...])
    acc = jnp.dot(s_hi, w_hi, preferred_element_type=F32)
    acc += jnp.dot(s_lo, w_hi, preferred_element_type=F32)
    acc += jnp.dot(s_hi, w_lo, preferred_element_type=F32)
    o_ref[...] = acc + b_ref[...]


def _ada_mods(cond, ada_w, ada_b, tn=512):
    depth, d, n = ada_w.shape
    rows = cond.shape[0]
    return pl.pallas_call(
        _ada_kernel,
        grid=(depth, n // tn),
        in_specs=[
            pl.BlockSpec((rows, d), lambda i, j: (0, 0)),
            pl.BlockSpec((None, d, tn), lambda i, j: (i, 0, j)),
            pl.BlockSpec((None, 1, tn), lambda i, j: (i, 0, j)),
        ],
        out_specs=pl.BlockSpec((None, rows, tn), lambda i, j: (i, 0, j)),
        out_shape=jax.ShapeDtypeStruct((depth, rows, n), F32),
        compiler_params=_cparams(("parallel", "parallel")),
        name="ada_mods",
    )(cond, ada_w, ada_b.reshape(depth, 1, n))


def _proj_kernel(x_ref, nw_ref, mod_ref, w_ref, *rest, d, has_tail):
    if has_tail:
        wt_ref, o_ref, ot_ref, h_ref = rest
    else:
        o_ref, h_ref = rest

    @pl.when(pl.program_id(1) == 0)
    def _():
        x = x_ref[...]
        y = x * lax.rsqrt(jnp.mean(x * x, axis=-1, keepdims=True) + EPS) * nw_ref[...]
        shift = mod_ref[0, :, 0:d]
        scale = mod_ref[0, :, d:2 * d]
        h = (y * (1.0 + scale) + shift).astype(BF16)
        h_ref[...] = h
        if has_tail:
            ot_ref[...] = jnp.dot(h, wt_ref[...].astype(BF16), preferred_element_type=F32)

    o_ref[...] = jnp.dot(h_ref[...], w_ref[...].astype(BF16), preferred_element_type=F32)


def _proj(x2, seq_len, mods, norm_w, w, n_main, w_tail=None, tm=1024, tn=512):
    t, d = x2.shape
    tm = min(tm, t)
    per_batch = mods.shape[0] > 1
    if per_batch:
        tm = math.gcd(tm, seq_len)
    mod_idx = (lambda i, j: ((i * tm) // seq_len, 0, 0)) if per_batch else (lambda i, j: (0, 0, 0))
    has_tail = w_tail is not None
    in_specs = [
        pl.BlockSpec((tm, d), lambda i, j: (i, 0)),
        pl.BlockSpec((1, d), lambda i, j: (0, 0)),
        pl.BlockSpec((1, 1, 3 * d), mod_idx),
        pl.BlockSpec((d, tn), lambda i, j: (0, j)),
    ]
    out_specs = [pl.BlockSpec((tm, tn), lambda i, j: (i, j))]
    out_shape = [jax.ShapeDtypeStruct((t, n_main), F32)]
    args = [x2, norm_w.reshape(1, d), mods, w]
    if has_tail:
        in_specs.append(pl.BlockSpec((d, LANES), lambda i, j: (0, 0)))
        out_specs.append(pl.BlockSpec((tm, LANES), lambda i, j: (i, 0)))
        out_shape.append(jax.ShapeDtypeStruct((t, LANES), F32))
        args.append(w_tail)
    outs = pl.pallas_call(
        functools.partial(_proj_kernel, d=d, has_tail=has_tail),
        grid=(t // tm, n_main // tn),
        in_specs=in_specs,
        out_specs=out_specs,
        out_shape=out_shape,
        scratch_shapes=[pltpu.VMEM((tm, d), BF16)],
        compiler_params=_cparams(("parallel", "arbitrary")),
        name="proj",
    )(*args)
    return outs if has_tail else outs[0]


def _out_kernel(a_ref, w_ref, x_ref, g_ref, o_ref):
    y = jnp.dot(a_ref[...], w_ref[...].astype(BF16), preferred_element_type=F32)
    o_ref[...] = x_ref[...] + g_ref[0] * y


def _out_proj(act, w, x2, seq_len, mods, tm=1024, tn=256):
    t, d = x2.shape
    k = act.shape[1]
    tm = min(tm, t)
    per_batch = mods.shape[0] > 1
    if per_batch:
        tm = math.gcd(tm, seq_len)
    gate_blk = 2 * d // tn
    mod_idx = ((lambda i, j: ((i * tm) // seq_len, 0, gate_blk + j)) if per_batch
               else (lambda i, j: (0, 0, gate_blk + j)))
    return pl.pallas_call(
        _out_kernel,
        grid=(t // tm, d // tn),
        in_specs=[
            pl.BlockSpec((tm, k), lambda i, j: (i, 0)),
            pl.BlockSpec((k, tn), lambda i, j: (0, j)),
            pl.BlockSpec((tm, tn), lambda i, j: (i, j)),
            pl.BlockSpec((1, 1, tn), mod_idx),
        ],
        out_specs=pl.BlockSpec((tm, tn), lambda i, j: (i, j)),
        out_shape=jax.ShapeDtypeStruct((t, d), F32),
        compiler_params=_cparams(("parallel", "arbitrary")),
        name="out_proj",
    )(act, w, x2, mods)


def _conv_rows(pad_ref, w_ref, r0, rb, taps, dil, pad_al):
    half = (taps - 1) // 2
    base = pad_al - half * dil
    acc = None
    if dil % SUBLANES == 0 and base % SUBLANES == 0:
        for k in range(taps):
            v = pad_ref[pl.ds(pl.multiple_of(r0 + base + k * dil, SUBLANES), rb), :]
            term = v * w_ref[k:k + 1, :]
            acc = term if acc is None else acc + term
    else:
        lo = (base // SUBLANES) * SUBLANES
        span = -(-(base - lo + (taps - 1) * dil + rb) // SUBLANES) * SUBLANES
        win = pad_ref[pl.ds(pl.multiple_of(r0 + lo, SUBLANES), span), :]
        for k in range(taps):
            off = base - lo + k * dil
            term = win[off:off + rb, :] * w_ref[k:k + 1, :]
            acc = term if acc is None else acc + term
    return acc


def _dnconv_kernel(x_ref, w_ref, b_ref, o_ref, pad_ref, *, nb, seq, rb, pad_al, n_norm_blocks):
    tc = pad_ref.shape[1]
    zeros = jnp.zeros((pad_al, tc), F32)
    pad_ref[0:pad_al, :] = zeros
    pad_ref[pad_al + seq:pad_al + seq + pad_al, :] = zeros
    is_qk = pl.program_id(1) < n_norm_blocks
    bias = b_ref[...]

    def seq_body(s, carry):
        pad_ref[pad_al:pad_al + seq, :] = x_ref[s]

        def chunk_body(ci, carry2):
            r0 = pl.multiple_of(ci * rb, rb)
            acc = _conv_rows(pad_ref, w_ref, r0, rb, DN_CONV_W, 1, pad_al)
            y = _silu(acc + bias)
            for g in range(tc // HEAD):
                yg = y[:, g * HEAD:(g + 1) * HEAD]
                nrm = yg * lax.rsqrt(jnp.sum(yg * yg, axis=-1, keepdims=True) + EPS)
                o_ref[g, s, pl.ds(r0, rb), :] = jnp.where(is_qk, nrm, yg)
            return carry2

        lax.fori_loop(0, seq // rb, chunk_body, 0)
        return carry

    lax.fori_loop(0, nb, seq_body, 0)


def _dn_conv(proj, nseq, seq, conv_w, conv_b, n_ch, n_norm, nb, tc=256, rb=64):
    width = proj.shape[1]
    x3 = proj.reshape(nseq, seq, width)
    pad_al = SUBLANES
    wpad = jnp.zeros((SUBLANES, n_ch), F32).at[:DN_CONV_W].set(conv_w)
    rb = min(rb, seq)
    return pl.pallas_call(
        functools.partial(_dnconv_kernel, nb=nb, seq=seq, rb=rb, pad_al=pad_al,
                          n_norm_blocks=n_norm // tc),
        grid=(nseq // nb, n_ch // tc),
        in_specs=[
            pl.BlockSpec((nb, seq, tc), lambda s, j: (s, 0, j)),
            pl.BlockSpec((SUBLANES, tc), lambda s, j: (0, j)),
            pl.BlockSpec((1, tc), lambda s, j: (0, j)),
        ],
        out_specs=pl.BlockSpec((tc // HEAD, nb, seq, HEAD), lambda s, j: (j, s, 0, 0)),
        out_shape=jax.ShapeDtypeStruct((n_ch // HEAD, nseq, seq, HEAD), F32),
        scratch_shapes=[pltpu.VMEM((seq + 2 * pad_al, tc), F32)],
        compiler_params=_cparams(("parallel", "parallel")),
        name="dn_conv",
    )(x3, wpad, conv_b.reshape(1, n_ch))


def _cfconv_kernel(a_ref, b_ref, w_ref, bias_ref, o_ref, pad_ref, *, nb, seq, rb, dil, pad_al):
    tc = pad_ref.shape[1]
    zeros = jnp.zeros((pad_al, tc), F32)
    pad_ref[0:pad_al, :] = zeros
    pad_ref[pad_al + seq:pad_al + seq + pad_al, :] = zeros
    bias = bias_ref[...]

    def seq_body(s, carry):
        a = a_ref[s]
        pad_ref[pad_al:pad_al + seq, :] = a * jax.nn.sigmoid(b_ref[s])

        def chunk_body(ci, carry2):
            r0 = pl.multiple_of(ci * rb, rb)
            acc = _conv_rows(pad_ref, w_ref, r0, rb, CF_CONV_W, dil, pad_al)
            o_ref[s, pl.ds(r0, rb), :] = acc + bias
            return carry2

        lax.fori_loop(0, seq // rb, chunk_body, 0)
        return carry

    lax.fori_loop(0, nb, seq_body, 0)


def _cf_conv(proj, nseq, seq, conv_w, conv_b, width, dil, nb, tc, rb=64):
    x3 = proj.reshape(nseq, seq, proj.shape[1])
    half = (CF_CONV_W - 1) // 2
    pad_al = -(-(half * dil) // SUBLANES) * SUBLANES
    wrows = -(-CF_CONV_W // SUBLANES) * SUBLANES
    wpad = jnp.zeros((wrows, width), F32).at[:CF_CONV_W].set(conv_w)
    rb = min(rb, seq)
    nblk = width // tc
    out = pl.pallas_call(
        functools.partial(_cfconv_kernel, nb=nb, seq=seq, rb=rb, dil=dil, pad_al=pad_al),
        grid=(nseq // nb, nblk),
        in_specs=[
            pl.BlockSpec((nb, seq, tc), lambda s, j: (s, 0, j)),
            pl.BlockSpec((nb, seq, tc), lambda s, j: (s, 0, nblk + j)),
            pl.BlockSpec((wrows, tc), lambda s, j: (0, j)),
            pl.BlockSpec((1, tc), lambda s, j: (0, j)),
        ],
        out_specs=pl.BlockSpec((nb, seq, tc), lambda s, j: (s, 0, j)),
        out_shape=jax.ShapeDtypeStruct((nseq, seq, width), F32),
        scratch_shapes=[pltpu.VMEM((seq + 2 * pad_al, tc), F32)],
        compiler_params=_cparams(("parallel", "parallel")),
        name="cf_conv",
    )(x3, x3, wpad, conv_b.reshape(1, width))
    return out.reshape(nseq * seq, width)


def _cfgate_kernel(u_ref, z_ref, w_ref, b_ref, o_ref):
    u = u_ref[...]
    mu = jnp.mean(u, axis=-1, keepdims=True)
    uc = u - mu
    var = jnp.mean(uc * uc, axis=-1, keepdims=True)
    y = uc * lax.rsqrt(var + EPS) * w_ref[...] + b_ref[...]
    o_ref[...] = (_silu(y) * _silu(z_ref[...])).astype(BF16)


def _cf_gate(uc, proj, ln_w, ln_b, tm=256):
    t, width = uc.shape
    tm = min(tm, t)
    return pl.pallas_call(
        _cfgate_kernel,
        grid=(t // tm,),
        in_specs=[
            pl.BlockSpec((tm, width), lambda i: (i, 0)),
            pl.BlockSpec((tm, width), lambda i: (i, 2)),
            pl.BlockSpec((1, width), lambda i: (0, 0)),
            pl.BlockSpec((1, width), lambda i: (0, 0)),
        ],
        out_specs=pl.BlockSpec((tm, width), lambda i: (i, 0)),
        out_shape=jax.ShapeDtypeStruct((t, width), BF16),
        compiler_params=_cparams(("parallel",)),
        name="cf_gate",
    )(uc, proj, ln_w.reshape(1, width), ln_b.reshape(1, width))


def _dngate_kernel(of_ref, ob_ref, z_ref, w_ref, o_ref, *, hb):
    for h in range(hb):
        o = of_ref[h] + ob_ref[h]
        o = o * lax.rsqrt(jnp.mean(o * o, axis=-1, keepdims=True) + EPS) * w_ref[...]
        z = z_ref[:, h * HEAD:(h + 1) * HEAD]
        o_ref[:, h * HEAD:(h + 1) * HEAD] = (o * _silu(z)).astype(BF16)


def _dn_gate(o_f, o_b, proj, z_col0, norm_w, tm=512, hb=4):
    nh, t, _ = o_f.shape
    tm = min(tm, t)
    zblk = z_col0 // (hb * HEAD)
    return pl.pallas_call(
        functools.partial(_dngate_kernel, hb=hb),
        grid=(t // tm, nh // hb),
        in_specs=[
            pl.BlockSpec((hb, tm, HEAD), lambda i, j: (j, i, 0)),
            pl.BlockSpec((hb, tm, HEAD), lambda i, j: (j, i, 0)),
            pl.BlockSpec((tm, hb * HEAD), lambda i, j: (i, zblk + j)),
            pl.BlockSpec((1, HEAD), lambda i, j: (0, 0)),
        ],
        out_specs=pl.BlockSpec((tm, hb * HEAD), lambda i, j: (i, j)),
        out_shape=jax.ShapeDtypeStruct((t, nh * HEAD), BF16),
        compiler_params=_cparams(("parallel", "parallel")),
        name="dn_gate",
    )(o_f, o_b, proj, norm_w.reshape(1, HEAD))


def _unit_tri_inverse(a, row, col):
    eye = (row == col).astype(F32)

    def same_block(shift):
        return lax.shift_right_logical(row, shift) == lax.shift_right_logical(col, shift)

    x = eye - jnp.where(same_block(1), a, 0.0)
    shift = 1
    while (1 << shift) < CHUNK:
        lb = jnp.where(same_block(shift + 1) & jnp.logical_not(same_block(shift)), a, 0.0)
        lx = jnp.dot(lb, x, precision=HI, preferred_element_type=F32)
        x = x - jnp.dot(x, lx, precision=HI, preferred_element_type=F32)
        shift += 1
    return x


def _gdr_kernel(*refs, hb, n_chunks, has_s0):
    (qf_ref, kf_ref, vf_ref, abf_ref, qb_ref, kb_ref, vb_ref, abb_ref, alog_ref, dtb_ref) = refs[:10]
    if has_s0:
        s0_ref = refs[10]
        of_ref, ob_ref, sout_ref, s_scr = refs[11:]
    else:
        of_ref, ob_ref, sout_ref, s_scr = refs[10:]
    n = pl.program_id(2)
    hblk = pl.program_id(1)

    @pl.when(n == 0)
    def _():
        if has_s0:
            s_scr[...] = s0_ref[0]
        else:
            s_scr[...] = jnp.zeros(s_scr.shape, F32)

    row = lax.broadcasted_iota(jnp.int32, (CHUNK, CHUNK), 0)
    col = lax.broadcasted_iota(jnp.int32, (CHUNK, CHUNK), 1)
    eye_l = (lax.broadcasted_iota(jnp.int32, (LANES, LANES), 0)
             == lax.broadcasted_iota(jnp.int32, (LANES, LANES), 1)).astype(F32)
    shift = (LANES - hblk * (hb * 8)) % LANES
    scale = HEAD ** -0.5

    for d, (q_ref, k_ref, v_ref, ab_ref, o_ref) in enumerate(
            ((qf_ref, kf_ref, vf_ref, abf_ref, of_ref), (qb_ref, kb_ref, vb_ref, abb_ref, ob_ref))):
        ab = ab_ref[...]
        x = ab + dtb_ref[...]
        softplus = jnp.maximum(x, 0.0) + jnp.log1p(jnp.exp(-jnp.abs(x)))
        g_all = pltpu.roll(-jnp.exp(alog_ref[...]) * softplus, shift, 1)
        beta_all = pltpu.roll(jax.nn.sigmoid(ab), shift, 1)
        if d == 0:
            incl, strict = row >= col, row > col
        else:
            incl, strict = row <= col, row < col
        gam = jnp.dot(incl.astype(F32), g_all, precision=HI, preferred_element_type=F32)
        gam_t = lax.dot_general(eye_l, gam, NT_DIMS, precision=HI, preferred_element_type=F32)
        last = CHUNK - 1 if d == 0 else 0
        for i in range(hb):
            k = k_ref[i]
            qs = q_ref[i] * scale
            kq = jnp.concatenate([k, qs], axis=0).astype(BF16)
            gq = lax.dot_general(kq, k.astype(BF16), NT_DIMS, preferred_element_type=F32)
            gmat, qk = gq[:CHUNK], gq[CHUNK:]
            for r in range(2):
                lane_b = i * 8 + d * 4 + r
                lane_g = lane_b + 2
                cg = gam[:, lane_g:lane_g + 1]
                cb = beta_all[:, lane_b:lane_b + 1]
                rg = gam_t[lane_g:lane_g + 1, :]
                gl = rg[:, last:last + 1]
                dec = jnp.exp(jnp.where(incl, cg - rg, -jnp.inf))
                a = jnp.where(strict, gmat * dec, 0.0) * cb
                tinv = _unit_tri_inverse(a, row, col)
                eg = jnp.exp(cg)
                v = v_ref[2 * i + r]
                rhs = jnp.concatenate([v * cb, k * (cb * eg)], axis=1).astype(BF16)
                uw = jnp.dot(tinv.astype(BF16), rhs, preferred_element_type=F32)
                u, w = uw[:, :HEAD], uw[:, HEAD:]
                attn = (qk * dec).astype(BF16)
                s = s_scr[d, 2 * i + r]
                wq = jnp.concatenate([w, qs * eg], axis=0).astype(BF16)
                wqs = jnp.dot(wq, s.astype(BF16), preferred_element_type=F32)
                v_new = (u - wqs[:CHUNK]).astype(BF16)
                o_ref[2 * i + r] = wqs[CHUNK:] + jnp.dot(attn, v_new, preferred_element_type=F32)
                kd = (k * jnp.exp(gl - cg)).astype(BF16)
                s_scr[d, 2 * i + r] = s * jnp.exp(gl) + lax.dot_general(
                    kd, v_new, TN_DIMS, preferred_element_type=F32)

    @pl.when(n == n_chunks - 1)
    def _():
        sout_ref[0] = s_scr[...]


def _gdr(qkv, ab, alog_row, dtb_row, s0, nseq, seq, hb=2):
    ng, t, _ = qkv.shape
    nqk = ng // 4
    nv = 2 * nqk
    n = seq // CHUNK
    has_s0 = s0 is not None
    kblk = nqk // hb
    fwd = lambda s, h, c: s * n + c
    bwd = lambda s, h, c: s * n + (n - 1 - c)

    def specs(rb):
        return [
            pl.BlockSpec((hb, CHUNK, HEAD), lambda s, h, c: (h, rb(s, h, c), 0)),
            pl.BlockSpec((hb, CHUNK, HEAD), lambda s, h, c: (kblk + h, rb(s, h, c), 0)),
            pl.BlockSpec((2 * hb, CHUNK, HEAD), lambda s, h, c: (kblk + h, rb(s, h, c), 0)),
            pl.BlockSpec((CHUNK, LANES), lambda s, h, c: (rb(s, h, c), 0)),
        ]

    in_specs = specs(fwd) + specs(bwd) + [
        pl.BlockSpec((1, LANES), lambda s, h, c: (0, 0)),
        pl.BlockSpec((1, LANES), lambda s, h, c: (0, 0)),
    ]
    args = [qkv, qkv, qkv, ab, qkv, qkv, qkv, ab, alog_row, dtb_row]
    if has_s0:
        in_specs.append(pl.BlockSpec((1, 2, 2 * hb, HEAD, HEAD), lambda s, h, c: (s, 0, h, 0, 0)))
        args.append(s0)
    o_f, o_b, s_new = pl.pallas_call(
        functools.partial(_gdr_kernel, hb=hb, n_chunks=n, has_s0=has_s0),
        grid=(nseq, nqk // hb, n),
        in_specs=in_specs,
        out_specs=[
            pl.BlockSpec((2 * hb, CHUNK, HEAD), lambda s, h, c: (h, fwd(s, h, c), 0)),
            pl.BlockSpec((2 * hb, CHUNK, HEAD), lambda s, h, c: (h, bwd(s, h, c), 0)),
            pl.BlockSpec((1, 2, 2 * hb, HEAD, HEAD), lambda s, h, c: (s, 0, h, 0, 0)),
        ],
        out_shape=[
            jax.ShapeDtypeStruct((nv, t, HEAD), F32),
            jax.ShapeDtypeStruct((nv, t, HEAD), F32),
            jax.ShapeDtypeStruct((nseq, 2, nv, HEAD, HEAD), F32),
        ],
        scratch_shapes=[pltpu.VMEM((2, 2 * hb, HEAD, HEAD), F32)],
        compiler_params=_cparams(("parallel", "parallel", "arbitrary")),
        name="gated_delta",
    )(*args)
    return o_f, o_b, s_new


def _rms_kernel(x_ref, w_ref, o_ref):
    x = x_ref[...]
    o_ref[...] = x * lax.rsqrt(jnp.mean(x * x, axis=-1, keepdims=True) + EPS) * w_ref[...]


def _final_norm(x2, w, tm=512):
    t, d = x2.shape
    tm = min(tm, t)
    return pl.pallas_call(
        _rms_kernel,
        grid=(t // tm,),
        in_specs=[pl.BlockSpec((tm, d), lambda i: (i, 0)), pl.BlockSpec((1, d), lambda i: (0, 0))],
        out_specs=pl.BlockSpec((tm, d), lambda i: (i, 0)),
        out_shape=jax.ShapeDtypeStruct((t, d), F32),
        compiler_params=_cparams(("parallel",)),
        name="final_norm",
    )(x2, w.reshape(1, d))


def _permute_ab_cols(w_tail, nqk):
    lead = w_tail.shape[:-1]
    w = w_tail.reshape(lead + (2, 2, nqk, 2))
    w = jnp.moveaxis(w, -2, -4)
    return w.reshape(lead + (8 * nqk,))


def _alpha_row(p, nqk):
    full = jnp.stack([jnp.zeros_like(p), p], axis=1)
    return _permute_ab_cols(full.reshape(1, -1), nqk)


def _dn_layer(x, mods, norm_w, w_in, conv_w, conv_b, a_log, dt_bias, head_norm_w, w_out, s0, nb_conv):
    b, seq, d = x.shape
    x2 = x.reshape(b * seq, d)
    val_dim = w_out.shape[0]
    nv = val_dim // HEAD
    nqk = nv // 2
    key_dim = nqk * HEAD
    conv_ch = 2 * key_dim + val_dim
    n_main = conv_ch + val_dim
    w_tail = _permute_ab_cols(w_in[:, n_main:], nqk)
    proj, ab = _proj(x2, seq, mods, norm_w, w_in, n_main, w_tail)
    qkv = _dn_conv(proj, b, seq, conv_w, conv_b, conv_ch, 2 * key_dim, nb_conv)
    qkv = qkv.reshape(conv_ch // HEAD, b * seq, HEAD)
    o_f, o_b, s_new = _gdr(qkv, ab, _alpha_row(a_log, nqk), _alpha_row(dt_bias, nqk), s0, b, seq)
    act = _dn_gate(o_f, o_b, proj, conv_ch, head_norm_w)
    xn = _out_proj(act, w_out, x2, seq, mods)
    return xn.reshape(b, seq, d), s_new


def _cf_layer(x, mods, norm_w, w_in, conv_w, conv_b, ln_w, ln_b, w_out, mode):
    b, seq, d = x.shape
    x2 = x.reshape(b * seq, d)
    width = w_out.shape[0]
    proj = _proj(x2, seq, mods, norm_w, w_in, 3 * width)
    if mode == "seq":
        uc = _cf_conv(proj, b, seq, conv_w, conv_b, width, 1, nb=min(4, b), tc=256)
    elif mode == "row":
        nseq = b * seq // GRID_W
        uc = _cf_conv(proj, nseq, GRID_W, conv_w, conv_b, width, 1, nb=min(16, nseq), tc=256)
    else:
        uc = _cf_conv(proj, b, seq, conv_w, conv_b, width, GRID_W, nb=1, tc=128)
    act = _cf_gate(uc, proj, ln_w, ln_b)
    xn = _out_proj(act, w_out, x2, seq, mods)
    return xn.reshape(b, seq, d)


def _to_col_major(x):
    b, seq, ch = x.shape
    rows = seq // GRID_W
    return x.reshape(b, rows, GRID_W, ch).transpose(0, 2, 1, 3).reshape(b, seq, ch)


def _to_row_major(x):
    b, seq, ch = x.shape
    rows = seq // GRID_W
    return x.reshape(b, GRID_W, rows, ch).transpose(0, 2, 1, 3).reshape(b, seq, ch)


def kernel(x_prompt, x_sample, state_l0, state_l2, c, c_ctx, ada_w, ada_b, norm_w, dn_w_in, dn_conv_w,
           dn_conv_b, dn_a_log, dn_dt_bias, dn_norm_w, dn_w_out, cf_w_in, cf_conv_w, cf_conv_b, cf_ln_w,
           cf_ln_b, cf_w_out, final_norm_w):
    depth = ada_w.shape[0]
    bs = x_sample.shape[0]
    d = x_prompt.shape[-1]
    rows = -(-(1 + bs) // SUBLANES) * SUBLANES
    cond = jnp.zeros((rows, d), F32).at[0].set(c_ctx).at[1:1 + bs].set(c)
    mods = _ada_mods(cond, ada_w, ada_b)
    caches = (state_l0, state_l2)
    new_states = []
    xp, xs = x_prompt, x_sample
    for i in range(depth):
        j = i // 2
        mp = mods[i, 0:1][:, None, :]
        ms = mods[i, 1:1 + bs][:, None, :]
        if i % 2 == 0:
            dn = (dn_w_in[j], dn_conv_w[j], dn_conv_b[j], dn_a_log[j], dn_dt_bias[j], dn_norm_w[j], dn_w_out[j])
            xp, sp = _dn_layer(xp, mp, norm_w[i], *dn, s0=None, nb_conv=min(8, xp.shape[0]))
            new_states.append(sp)
            if j % 2 == 1:
                xs_cm, _ = _dn_layer(_to_col_major(xs), ms, norm_w[i], *dn, s0=caches[j], nb_conv=1)
                xs = _to_row_major(xs_cm)
            else:
                xs, _ = _dn_layer(xs, ms, norm_w[i], *dn, s0=caches[j], nb_conv=1)
        else:
            cf = (cf_w_in[j], cf_conv_w[j], cf_conv_b[j], cf_ln_w[j], cf_ln_b[j], cf_w_out[j])
            xp = _cf_layer(xp, mp, norm_w[i], *cf, mode="seq")
            xs = _cf_layer(xs, ms, norm_w[i], *cf, mode="row" if j % 2 == 0 else "col")
    bp, sp_len, _ = xp.shape
    y_prompt = _final_norm(xp.reshape(bp * sp_len, d), final_norm_w).reshape(xp.shape)
    y_sample = _final_norm(xs.reshape(bs * xs.shape[1], d), final_norm_w).reshape(xs.shape)
    return (y_prompt, y_sample, new_states[0], new_states[1])
```

```python
import functools
import math

import jax
import jax.numpy as jnp
from jax import lax
from jax.experimental import pallas as pl
from jax.experimental.pallas import tpu as pltpu

F32 = jnp.float32
BF16 = jnp.bfloat16
EPS = 1e-6
HI = lax.Precision.HIGHEST

LANES = 128
SUBLANES = 8
VMEM_LIMIT_BYTES = 48 * 1024 * 1024

GRID_W = 64
HEAD = 128
CHUNK = 64
PAIR = 2 * CHUNK
DN_CONV_W = 5
CF_CONV_W = 31
NT_DIMS = (((1,), (1,)), ((), ()))


def _cparams(sem):
    return pltpu.CompilerParams(dimension_semantics=sem, vmem_limit_bytes=VMEM_LIMIT_BYTES)


def _silu(x):
    return x * jax.nn.sigmoid(x)


def _split_bf16(x):
    hi = x.astype(BF16)
    lo = (x - hi.astype(F32)).astype(BF16)
    return hi, lo


def _ada_kernel(c_ref, w_ref, b_ref, o_ref):
    c = c_ref[...]
    s_hi, s_lo = _split_bf16(_silu(c))
    w_hi, w_lo = _split_bf16(w_ref[...])
    acc = jnp.dot(s_hi, w_hi, preferred_element_type=F32)
    acc += jnp.dot(s_lo, w_hi, preferred_element_type=F32)
    acc += jnp.dot(s_hi, w_lo, preferred_element_type=F32)
    o_ref[...] = acc + b_ref[...]


def _ada_mods(cond, ada_w, ada_b, tn=512):
    depth, d, n = ada_w.shape
    rows = cond.shape[0]
    return pl.pallas_call(
        _ada_kernel,
        grid=(depth, n // tn),
        in_specs=[
            pl.BlockSpec((rows, d), lambda i, j: (0, 0)),
            pl.BlockSpec((None, d, tn), lambda i, j: (i, 0, j)),
            pl.BlockSpec((None, 1, tn), lambda i, j: (i, 0, j)),
        ],
        out_specs=pl.BlockSpec((None, rows, tn), lambda i, j: (i, 0, j)),
        out_shape=jax.ShapeDtypeStruct((depth, rows, n), F32),
        compiler_params=_cparams(("parallel", "parallel")),
        name="ada_mods",
    )(cond, ada_w, ada_b.reshape(depth, 1, n))


def _proj_kernel(x_ref, nw_ref, mod_ref, w_ref, *rest, d, has_tail):
    if has_tail:
        wt_ref, o_ref, ot_ref, h_ref = rest
    else:
        o_ref, h_ref = rest

    @pl.when(pl.program_id(1) == 0)
    def _():
        x = x_ref[...]
        y = x * lax.rsqrt(jnp.mean(x * x, axis=-1, keepdims=True) + EPS) * nw_ref[...]
        shift = mod_ref[0, :, 0:d]
        scale = mod_ref[0, :, d:2 * d]
        h = (y * (1.0 + scale) + shift).astype(BF16)
        h_ref[...] = h
        if has_tail:
            ot_ref[...] = jnp.dot(h, wt_ref[...].astype(BF16), preferred_element_type=F32)

    o_ref[...] = jnp.dot(h_ref[...], w_ref[...].astype(BF16), preferred_element_type=F32)


def _proj(x2, seq_len, mods, norm_w, w, n_main, w_tail=None, tm=1024, tn=512):
    t, d = x2.shape
    tm = min(tm, t)
    per_batch = mods.shape[0] > 1
    if per_batch:
        tm = math.gcd(tm, seq_len)
    mod_idx = (lambda i, j: ((i * tm) // seq_len, 0, 0)) if per_batch else (lambda i, j: (0, 0, 0))
    has_tail = w_tail is not None
    in_specs = [
        pl.BlockSpec((tm, d), lambda i, j: (i, 0)),
        pl.BlockSpec((1, d), lambda i, j: (0, 0)),
        pl.BlockSpec((1, 1, 3 * d), mod_idx),
        pl.BlockSpec((d, tn), lambda i, j: (0, j)),
    ]
    out_specs = [pl.BlockSpec((tm, tn), lambda i, j: (i, j))]
    out_shape = [jax.ShapeDtypeStruct((t, n_main), F32)]
    args = [x2, norm_w.reshape(1, d), mods, w]
    if has_tail:
        in_specs.append(pl.BlockSpec((d, LANES), lambda i, j: (0, 0)))
        out_specs.append(pl.BlockSpec((tm, LANES), lambda i, j: (i, 0)))
        out_shape.append(jax.ShapeDtypeStruct((t, LANES), F32))
        args.append(w_tail)
    outs = pl.pallas_call(
        functools.partial(_proj_kernel, d=d, has_tail=has_tail),
        grid=(t // tm, n_main // tn),
        in_specs=in_specs,
        out_specs=out_specs,
        out_shape=out_shape,
        scratch_shapes=[pltpu.VMEM((tm, d), BF16)],
        compiler_params=_cparams(("parallel", "arbitrary")),
        name="proj",
    )(*args)
    return outs if has_tail else outs[0]


def _out_kernel(a_ref, w_ref, x_ref, g_ref, o_ref):
    y = jnp.dot(a_ref[...], w_ref[...].astype(BF16), preferred_element_type=F32)
    o_ref[...] = x_ref[...] + g_ref[0] * y


def _out_proj(act, w, x2, seq_len, mods, tm=1024, tn=256):
    t, d = x2.shape
    k = act.shape[1]
    tm = min(tm, t)
    per_batch = mods.shape[0] > 1
    if per_batch:
        tm = math.gcd(tm, seq_len)
    gate_blk = 2 * d // tn
    mod_idx = ((lambda i, j: ((i * tm) // seq_len, 0, gate_blk + j)) if per_batch
               else (lambda i, j: (0, 0, gate_blk + j)))
    return pl.pallas_call(
        _out_kernel,
        grid=(t // tm, d // tn),
        in_specs=[
            pl.BlockSpec((tm, k), lambda i, j: (i, 0)),
            pl.BlockSpec((k, tn), lambda i, j: (0, j)),
            pl.BlockSpec((tm, tn), lambda i, j: (i, j)),
            pl.BlockSpec((1, 1, tn), mod_idx),
        ],
        out_specs=pl.BlockSpec((tm, tn), lambda i, j: (i, j)),
        out_shape=jax.ShapeDtypeStruct((t, d), F32),
        compiler_params=_cparams(("parallel", "arbitrary")),
        name="out_proj",
    )(act, w, x2, mods)


def _conv_rows(pad_ref, w_ref, r0, rb, taps, dil, pad_al):
    half = (taps - 1) // 2
    base = pad_al - half * dil
    acc = None
    if dil % SUBLANES == 0 and base % SUBLANES == 0:
        for k in range(taps):
            v = pad_ref[pl.ds(pl.multiple_of(r0 + base + k * dil, SUBLANES), rb), :]
            term = v * w_ref[k:k + 1, :]
            acc = term if acc is None else acc + term
    else:
        lo = (base // SUBLANES) * SUBLANES
        span = -(-(base - lo + (taps - 1) * dil + rb) // SUBLANES) * SUBLANES
        win = pad_ref[pl.ds(pl.multiple_of(r0 + lo, SUBLANES), span), :]
        for k in range(taps):
            off = base - lo + k * dil
            term = win[off:off + rb, :] * w_ref[k:k + 1, :]
            acc = term if acc is None else acc + term
    return acc


def _dnconv_kernel(x_ref, w_ref, b_ref, o_ref, pad_ref, *, nb, seq, rb, pad_al, n_norm_blocks):
    tc = pad_ref.shape[1]
    zeros = jnp.zeros((pad_al, tc), F32)
    pad_ref[0:pad_al, :] = zeros
    pad_ref[pad_al + seq:pad_al + seq + pad_al, :] = zeros
    is_qk = pl.program_id(1) < n_norm_blocks
    bias = b_ref[...]

    def seq_body(s, carry):
        pad_ref[pad_al:pad_al + seq, :] = x_ref[s]

        def chunk_body(ci, carry2):
            r0 = pl.multiple_of(ci * rb, rb)
            acc = _conv_rows(pad_ref, w_ref, r0, rb, DN_CONV_W, 1, pad_al)
            y = _silu(acc + bias)
            for g in range(tc // HEAD):
                yg = y[:, g * HEAD:(g + 1) * HEAD]
                nrm = yg * lax.rsqrt(jnp.sum(yg * yg, axis=-1, keepdims=True) + EPS)
                o_ref[g, s, pl.ds(r0, rb), :] = jnp.where(is_qk, nrm, yg)
            return carry2

        lax.fori_loop(0, seq // rb, chunk_body, 0)
        return carry

    lax.fori_loop(0, nb, seq_body, 0)


def _dn_conv(proj, nseq, seq, conv_w, conv_b, n_ch, n_norm, nb, tc=256, rb=64):
    width = proj.shape[1]
    x3 = proj.reshape(nseq, seq, width)
    pad_al = SUBLANES
    wpad = jnp.zeros((SUBLANES, n_ch), F32).at[:DN_CONV_W].set(conv_w)
    rb = min(rb, seq)
    return pl.pallas_call(
        functools.partial(_dnconv_kernel, nb=nb, seq=seq, rb=rb, pad_al=pad_al,
                          n_norm_blocks=n_norm // tc),
        grid=(nseq // nb, n_ch // tc),
        in_specs=[
            pl.BlockSpec((nb, seq, tc), lambda s, j: (s, 0, j)),
            pl.BlockSpec((SUBLANES, tc), lambda s, j: (0, j)),
            pl.BlockSpec((1, tc), lambda s, j: (0, j)),
        ],
        out_specs=pl.BlockSpec((tc // HEAD, nb, seq, HEAD), lambda s, j: (j, s, 0, 0)),
        out_shape=jax.ShapeDtypeStruct((n_ch // HEAD, nseq, seq, HEAD), F32),
        scratch_shapes=[pltpu.VMEM((seq + 2 * pad_al, tc), F32)],
        compiler_params=_cparams(("parallel", "parallel")),
        name="dn_conv",
    )(x3, wpad, conv_b.reshape(1, n_ch))


def _cfconv_kernel(a_ref, b_ref, w_ref, bias_ref, o_ref, pad_ref, *, nb, seq, rb, dil, pad_al):
    tc = pad_ref.shape[1]
    zeros = jnp.zeros((pad_al, tc), F32)
    pad_ref[0:pad_al, :] = zeros
    pad_ref[pad_al + seq:pad_al + seq + pad_al, :] = zeros
    bias = bias_ref[...]

    def seq_body(s, carry):
        a = a_ref[s]
        pad_ref[pad_al:pad_al + seq, :] = a * jax.nn.sigmoid(b_ref[s])

        def chunk_body(ci, carry2):
            r0 = pl.multiple_of(ci * rb, rb)
            acc = _conv_rows(pad_ref, w_ref, r0, rb, CF_CONV_W, dil, pad_al)
            o_ref[s, pl.ds(r0, rb), :] = acc + bias
            return carry2

        lax.fori_loop(0, seq // rb, chunk_body, 0)
        return carry

    lax.fori_loop(0, nb, seq_body, 0)


def _cf_conv(proj, nseq, seq, conv_w, conv_b, width, dil, nb, tc, rb=64):
    x3 = proj.reshape(nseq, seq, proj.shape[1])
    half = (CF_CONV_W - 1) // 2
    pad_al = -(-(half * dil) // SUBLANES) * SUBLANES
    wrows = -(-CF_CONV_W // SUBLANES) * SUBLANES
    wpad = jnp.zeros((wrows, width), F32).at[:CF_CONV_W].set(conv_w)
    rb = min(rb, seq)
    nblk = width // tc
    out = pl.pallas_call(
        functools.partial(_cfconv_kernel, nb=nb, seq=seq, rb=rb, dil=dil, pad_al=pad_al),
        grid=(nseq // nb, nblk),
        in_specs=[
            pl.BlockSpec((nb, seq, tc), lambda s, j: (s, 0, j)),
            pl.BlockSpec((nb, seq, tc), lambda s, j: (s, 0, nblk + j)),
            pl.BlockSpec((wrows, tc), lambda s, j: (0, j)),
            pl.BlockSpec((1, tc), lambda s, j: (0, j)),
        ],
        out_specs=pl.BlockSpec((nb, seq, tc), lambda s, j: (s, 0, j)),
        out_shape=jax.ShapeDtypeStruct((nseq, seq, width), F32),
        scratch_shapes=[pltpu.VMEM((seq + 2 * pad_al, tc), F32)],
        compiler_params=_cparams(("parallel", "parallel")),
        name="cf_conv",
    )(x3, x3, wpad, conv_b.reshape(1, width))
    return out.reshape(nseq * seq, width)


def _cfgate_kernel(u_ref, z_ref, w_ref, b_ref, o_ref):
    u = u_ref[...]
    mu = jnp.mean(u, axis=-1, keepdims=True)
    uc = u - mu
    var = jnp.mean(uc * uc, axis=-1, keepdims=True)
    y = uc * lax.rsqrt(var + EPS) * w_ref[...] + b_ref[...]
    o_ref[...] = (_silu(y) * _silu(z_ref[...])).astype(BF16)


def _cf_gate(uc, proj, ln_w, ln_b, tm=256):
    t, width = uc.shape
    tm = min(tm, t)
    return pl.pallas_call(
        _cfgate_kernel,
        grid=(t // tm,),
        in_specs=[
            pl.BlockSpec((tm, width), lambda i: (i, 0)),
            pl.BlockSpec((tm, width), lambda i: (i, 2)),
            pl.BlockSpec((1, width), lambda i: (0, 0)),
            pl.BlockSpec((1, width), lambda i: (0, 0)),
        ],
        out_specs=pl.BlockSpec((tm, width), lambda i: (i, 0)),
        out_shape=jax.ShapeDtypeStruct((t, width), BF16),
        compiler_params=_cparams(("parallel",)),
        name="cf_gate",
    )(uc, proj, ln_w.reshape(1, width), ln_b.reshape(1, width))


def _dngate_kernel(of_ref, ob_ref, z_ref, w_ref, o_ref, *, hb):
    for h in range(hb):
        o = of_ref[h] + ob_ref[h]
        o = o * lax.rsqrt(jnp.mean(o * o, axis=-1, keepdims=True) + EPS) * w_ref[...]
        z = z_ref[:, h * HEAD:(h + 1) * HEAD]
        o_ref[:, h * HEAD:(h + 1) * HEAD] = (o * _silu(z)).astype(BF16)


def _dn_gate(o_f, o_b, proj, z_col0, norm_w, tm=512, hb=4):
    nh, t, _ = o_f.shape
    tm = min(tm, t)
    zblk = z_col0 // (hb * HEAD)
    return pl.pallas_call(
        functools.partial(_dngate_kernel, hb=hb),
        grid=(t // tm, nh // hb),
        in_specs=[
            pl.BlockSpec((hb, tm, HEAD), lambda i, j: (j, i, 0)),
            pl.BlockSpec((hb, tm, HEAD), lambda i, j: (j, i, 0)),
            pl.BlockSpec((tm, hb * HEAD), lambda i, j: (i, zblk + j)),
            pl.BlockSpec((1, HEAD), lambda i, j: (0, 0)),
        ],
        out_specs=pl.BlockSpec((tm, hb * HEAD), lambda i, j: (i, j)),
        out_shape=jax.ShapeDtypeStruct((t, nh * HEAD), BF16),
        compiler_params=_cparams(("parallel", "parallel")),
        name="dn_gate",
    )(o_f, o_b, proj, norm_w.reshape(1, HEAD))


def _block_diag(a, b):
    za = jnp.zeros((a.shape[0], b.shape[1]), a.dtype)
    zb = jnp.zeros((b.shape[0], a.shape[1]), a.dtype)
    return jnp.concatenate([jnp.concatenate([a, za], axis=1), jnp.concatenate([zb, b], axis=1)], axis=0)


def _quad_unit_tri_inverse(a_f, a_b, prow, pcol):
    eye = (prow == pcol).astype(F32)

    def same_block(shift):
        return lax.shift_right_logical(prow, shift) == lax.shift_right_logical(pcol, shift)

    blk = same_block(1)
    x_f = eye - jnp.where(blk, a_f, 0.0)
    x_b = eye - jnp.where(blk, a_b, 0.0)
    shift = 1
    while (1 << shift) < CHUNK:
        lvl = same_block(shift + 1) & jnp.logical_not(same_block(shift))
        lq = _block_diag(jnp.where(lvl, a_f, 0.0).astype(BF16), jnp.where(lvl, a_b, 0.0).astype(BF16))
        xq = _block_diag(x_f.astype(BF16), x_b.astype(BF16))
        lx = jnp.dot(lq, xq, preferred_element_type=F32)
        lxq = _block_diag(lx[:PAIR, :PAIR].astype(BF16), lx[PAIR:, PAIR:].astype(BF16))
        xlx = jnp.dot(xq, lxq, preferred_element_type=F32)
        x_f = x_f - xlx[:PAIR, :PAIR]
        x_b = x_b - xlx[PAIR:, PAIR:]
        shift += 1
    return x_f, x_b


def _gdr_prep_kernel(q_ref, k_ref, v_ref, ab_ref, alog_ref, dtb_ref, wq_ref, u_ref, ak_ref,
                     gam_s, beta_s, gamt_s, *, hqb, nqk):
    row = lax.broadcasted_iota(jnp.int32, (CHUNK, CHUNK), 0)
    col = lax.broadcasted_iota(jnp.int32, (CHUNK, CHUNK), 1)
    prow = lax.broadcasted_iota(jnp.int32, (PAIR, PAIR), 0)
    pcol = lax.broadcasted_iota(jnp.int32, (PAIR, PAIR), 1)
    same_prob = lax.shift_right_logical(prow, 6) == lax.shift_right_logical(pcol, 6)
    incl = (same_prob & (prow >= pcol), same_prob & (prow <= pcol))
    strict = (same_prob & (prow > pcol), same_prob & (prow < pcol))
    eye_l = (prow == pcol).astype(F32)
    lane = lax.broadcasted_iota(jnp.int32, (CHUNK, LANES), 1)
    lane1 = lax.broadcasted_iota(jnp.int32, (1, PAIR), 1)
    scale = HEAD ** -0.5

    ab = ab_ref[...]
    x = ab + dtb_ref[...]
    softplus = jnp.maximum(x, 0.0) + jnp.log1p(jnp.exp(-jnp.abs(x)))
    g_all = -jnp.exp(alog_ref[...]) * softplus
    gam_f = jnp.dot((row >= col).astype(F32), g_all, precision=HI, preferred_element_type=F32)
    gam_b = jnp.dot((row <= col).astype(F32), g_all, precision=HI, preferred_element_type=F32)
    gam = jnp.where((lane & 4) == 0, gam_f, gam_b)
    gam_s[...] = gam
    beta_s[...] = jax.nn.sigmoid(ab)
    gam2 = jnp.concatenate([gam, pltpu.roll(gam, LANES - 1, 1)], axis=0)
    gamt_s[...] = lax.dot_general(eye_l, gam2, NT_DIMS, precision=HI, preferred_element_type=F32)

    def block_body(blk, carry):
        shift = (LANES - blk * (hqb * 8)) % LANES
        gam_r = pltpu.roll(gam_s[...], shift, 1)
        beta_r = pltpu.roll(beta_s[...], shift, 1)
        gam_t = gamt_s[pl.ds(pl.multiple_of(blk * (hqb * 8), hqb * 8), hqb * 8), :]
        for i in range(hqb):
            hq = blk * hqb + i
            k = k_ref[hq]
            qs = q_ref[hq] * scale
            k2 = jnp.concatenate([k, k], axis=0)
            k2b = k2.astype(BF16)
            gq = lax.dot_general(jnp.concatenate([k2b, qs.astype(BF16)], axis=0), k2b, NT_DIMS,
                                 preferred_element_type=F32)
            g2, qk2 = gq[:PAIR], gq[PAIR:]
            kt2 = k2.T
            q2 = jnp.concatenate([qs, qs], axis=0)
            v2 = jnp.concatenate([v_ref[2 * hq], v_ref[2 * hq + 1]], axis=0)
            a_m, rhs, qg, glrow = [], [], [], []
            for d in range(2):
                lane_b = i * 8 + d * 4
                lane_g = lane_b + 2
                cg = jnp.concatenate([gam_r[:, lane_g:lane_g + 1], gam_r[:, lane_g + 1:lane_g + 2]], axis=0)
                cb = jnp.concatenate([beta_r[:, lane_b:lane_b + 1], beta_r[:, lane_b + 1:lane_b + 2]], axis=0)
                rg = gam_t[lane_g:lane_g + 1, :]
                last = CHUNK - 1 if d == 0 else 0
                gl = jnp.where(lane1 < CHUNK, rg[:, last:last + 1], rg[:, CHUNK + last:CHUNK + last + 1])
                dec = jnp.exp(jnp.where(incl[d], cg - rg, -jnp.inf))
                a_m.append(jnp.where(strict[d], g2 * dec, 0.0) * cb)
                eg = jnp.exp(cg)
                rhs.append(jnp.concatenate([v2 * cb, k2 * (cb * eg)], axis=1).astype(BF16))
                qg.append(q2 * eg)
                glrow.append(gl)
                dsum = dec[:CHUNK] + dec[CHUNK:]
                ak_ref[d, 0, hq] = jnp.concatenate([qk2 * dsum, kt2 * jnp.exp(gl - rg)], axis=0).astype(BF16)
            t_f, t_b = _quad_unit_tri_inverse(a_m[0], a_m[1], prow, pcol)
            tq = _block_diag(t_f.astype(BF16), t_b.astype(BF16))
            uw = jnp.dot(tq, jnp.concatenate(rhs, axis=0), preferred_element_type=F32)
            for d in range(2):
                uw_d = uw[d * PAIR:(d + 1) * PAIR]
                u2, w2 = uw_d[:, :HEAD], uw_d[:, HEAD:]
                wq_a = jnp.concatenate([w2[:CHUNK], qg[d][:CHUNK]], axis=0)
                wq_b = jnp.concatenate([w2[CHUNK:], qg[d][CHUNK:]], axis=0)
                wq_ref[d, 0, hq] = jnp.concatenate([wq_a, wq_b], axis=1).astype(BF16)
                u_ref[d, 0, hq, 0:PAIR, :] = u2
                egl = jnp.exp(glrow[d])
                u_ref[d, 0, hq, PAIR:PAIR + SUBLANES, :] = jnp.broadcast_to(egl[:, 0:1], (SUBLANES, HEAD))
                u_ref[d, 0, hq, PAIR + SUBLANES:PAIR + 2 * SUBLANES, :] = jnp.broadcast_to(
                    egl[:, CHUNK:CHUNK + 1], (SUBLANES, HEAD))
        return carry

    lax.fori_loop(0, nqk // hqb, block_body, 0)


def _gdr_prep(qkv, ab, alog_row, dtb_row, hqb=4):
    ng, t, _ = qkv.shape
    nqk = ng // 4
    nv = 2 * nqk
    nc = t // CHUNK
    urows = PAIR + 2 * SUBLANES
    return pl.pallas_call(
        functools.partial(_gdr_prep_kernel, hqb=hqb, nqk=nqk),
        grid=(nc,),
        in_specs=[
            pl.BlockSpec((nqk, CHUNK, HEAD), lambda c: (0, c, 0)),
            pl.BlockSpec((nqk, CHUNK, HEAD), lambda c: (1, c, 0)),
            pl.BlockSpec((nv, CHUNK, HEAD), lambda c: (1, c, 0)),
            pl.BlockSpec((CHUNK, LANES), lambda c: (c, 0)),
            pl.BlockSpec((1, LANES), lambda c: (0, 0)),
            pl.BlockSpec((1, LANES), lambda c: (0, 0)),
        ],
        out_specs=[
            pl.BlockSpec((2, 1, nqk, PAIR, 2 * HEAD), lambda c: (0, c, 0, 0, 0)),
            pl.BlockSpec((2, 1, nqk, urows, HEAD), lambda c: (0, c, 0, 0, 0)),
            pl.BlockSpec((2, 1, nqk, CHUNK + HEAD, PAIR), lambda c: (0, c, 0, 0, 0)),
        ],
        out_shape=[
            jax.ShapeDtypeStruct((2, nc, nqk, PAIR, 2 * HEAD), BF16),
            jax.ShapeDtypeStruct((2, nc, nqk, urows, HEAD), F32),
            jax.ShapeDtypeStruct((2, nc, nqk, CHUNK + HEAD, PAIR), BF16),
        ],
        scratch_shapes=[pltpu.VMEM((CHUNK, LANES), F32), pltpu.VMEM((CHUNK, LANES), F32),
                        pltpu.VMEM((LANES, PAIR), F32)],
        compiler_params=_cparams(("parallel",)),
        name="gdr_prep",
    )(qkv, qkv, qkv, ab, alog_row, dtb_row)


def _gdr_scan_kernel(*refs, nqk, n_chunks, has_s0):
    wqf_ref, uf_ref, akf_ref, wqb_ref, ub_ref, akb_ref = refs[:6]
    if has_s0:
        s0_ref = refs[6]
        of_ref, ob_ref, sout_ref, s_scr = refs[7:]
    else:
        of_ref, ob_ref, sout_ref, s_scr = refs[6:]
    n = pl.program_id(1)

    @pl.when(n == 0)
    def _():
        if has_s0:
            s_scr[...] = s0_ref[0]
        else:
            s_scr[...] = jnp.zeros(s_scr.shape, F32)

    dirs = ((wqf_ref, uf_ref, of_ref), (wqb_ref, ub_ref, ob_ref))
    zero_v = jnp.zeros((CHUNK, HEAD), BF16)
    for hq in range(nqk):
        m1s, v_rows = [], []
        for d, (wq_ref, u_ref, _) in enumerate(dirs):
            s_a = s_scr[d, 2 * hq]
            s_b = s_scr[d, 2 * hq + 1]
            m1 = jnp.dot(wq_ref[0, 0, hq], _block_diag(s_a.astype(BF16), s_b.astype(BF16)),
                         preferred_element_type=F32)
            m1s.append(m1)
            for r in range(2):
                p = 2 * d + r
                v_new = (u_ref[0, 0, hq, r * CHUNK:(r + 1) * CHUNK, :]
                         - m1[:CHUNK, r * HEAD:(r + 1) * HEAD]).astype(BF16)
                v_rows.append(jnp.concatenate([v_new if c == p else zero_v for c in range(4)], axis=1))
        ak = jnp.concatenate([akf_ref[0, 0, hq], akb_ref[0, 0, hq]], axis=1)
        m2 = jnp.dot(ak, jnp.concatenate(v_rows, axis=0), preferred_element_type=F32)
        for d, (_, u_ref, o_ref) in enumerate(dirs):
            for r in range(2):
                p = 2 * d + r
                hv = 2 * hq + r
                m2_p = m2[:, p * HEAD:(p + 1) * HEAD]
                o_ref[hv] = m1s[d][CHUNK:, r * HEAD:(r + 1) * HEAD] + m2_p[:CHUNK]
                decay = u_ref[0, 0, hq, PAIR + r * SUBLANES:PAIR + r * SUBLANES + 1, :]
                s_scr[d, hv] = s_scr[d, hv] * decay + m2_p[CHUNK:]

    @pl.when(n == n_chunks - 1)
    def _():
        sout_ref[0] = s_scr[...]


def _gdr_scan(wq, u, ak, s0, nseq, seq):
    _, nc, nqk, _, _ = wq.shape
    nv = 2 * nqk
    n = seq // CHUNK
    t = nc * CHUNK
    urows = u.shape[3]
    has_s0 = s0 is not None
    fwd = lambda s, c: s * n + c
    bwd = lambda s, c: s * n + (n - 1 - c)

    def specs(d, cb):
        return [
            pl.BlockSpec((1, 1, nqk, PAIR, 2 * HEAD), lambda s, c: (d, cb(s, c), 0, 0, 0)),
            pl.BlockSpec((1, 1, nqk, urows, HEAD), lambda s, c: (d, cb(s, c), 0, 0, 0)),
            pl.BlockSpec((1, 1, nqk, CHUNK + HEAD, PAIR), lambda s, c: (d, cb(s, c), 0, 0, 0)),
        ]

    in_specs = specs(0, fwd) + specs(1, bwd)
    args = [wq, u, ak, wq, u, ak]
    if has_s0:
        in_specs.append(pl.BlockSpec((1, 2, nv, HEAD, HEAD), lambda s, c: (s, 0, 0, 0, 0)))
        args.append(s0)
    o_f, o_b, s_new = pl.pallas_call(
        functools.partial(_gdr_scan_kernel, nqk=nqk, n_chunks=n, has_s0=has_s0),
        grid=(nseq, n),
        in_specs=in_specs,
        out_specs=[
            pl.BlockSpec((nv, CHUNK, HEAD), lambda s, c: (0, fwd(s, c), 0)),
            pl.BlockSpec((nv, CHUNK, HEAD), lambda s, c: (0, bwd(s, c), 0)),
            pl.BlockSpec((1, 2, nv, HEAD, HEAD), lambda s, c: (s, 0, 0, 0, 0)),
        ],
        out_shape=[
            jax.ShapeDtypeStruct((nv, t, HEAD), F32),
            jax.ShapeDtypeStruct((nv, t, HEAD), F32),
            jax.ShapeDtypeStruct((nseq, 2, nv, HEAD, HEAD), F32),
        ],
        scratch_shapes=[pltpu.VMEM((2, nv, HEAD, HEAD), F32)],
        compiler_params=_cparams(("parallel", "arbitrary")),
        name="gdr_scan",
    )(*args)
    return o_f, o_b, s_new


def _gdr(qkv, ab, alog_row, dtb_row, s0, nseq, seq):
    wq, u, ak = _gdr_prep(qkv, ab, alog_row, dtb_row)
    return _gdr_scan(wq, u, ak, s0, nseq, seq)


def _rms_kernel(x_ref, w_ref, o_ref):
    x = x_ref[...]
    o_ref[...] = x * lax.rsqrt(jnp.mean(x * x, axis=-1, keepdims=True) + EPS) * w_ref[...]


def _final_norm(x2, w, tm=512):
    t, d = x2.shape
    tm = min(tm, t)
    return pl.pallas_call(
        _rms_kernel,
        grid=(t // tm,),
        in_specs=[pl.BlockSpec((tm, d), lambda i: (i, 0)), pl.BlockSpec((1, d), lambda i: (0, 0))],
        out_specs=pl.BlockSpec((tm, d), lambda i: (i, 0)),
        out_shape=jax.ShapeDtypeStruct((t, d), F32),
        compiler_params=_cparams(("parallel",)),
        name="final_norm",
    )(x2, w.reshape(1, d))


def _permute_ab_cols(w_tail, nqk):
    lead = w_tail.shape[:-1]
    w = w_tail.reshape(lead + (2, 2, nqk, 2))
    w = jnp.moveaxis(w, -2, -4)
    return w.reshape(lead + (8 * nqk,))


def _alpha_row(p, nqk):
    full = jnp.stack([jnp.zeros_like(p), p], axis=1)
    return _permute_ab_cols(full.reshape(1, -1), nqk)


def _dn_layer(x, mods, norm_w, w_in, conv_w, conv_b, a_log, dt_bias, head_norm_w, w_out, s0, nb_conv):
    b, seq, d = x.shape
    x2 = x.reshape(b * seq, d)
    val_dim = w_out.shape[0]
    nv = val_dim // HEAD
    nqk = nv // 2
    key_dim = nqk * HEAD
    conv_ch = 2 * key_dim + val_dim
    n_main = conv_ch + val_dim
    w_tail = _permute_ab_cols(w_in[:, n_main:], nqk)
    proj, ab = _proj(x2, seq, mods, norm_w, w_in, n_main, w_tail)
    qkv = _dn_conv(proj, b, seq, conv_w, conv_b, conv_ch, 2 * key_dim, nb_conv)
    qkv = qkv.reshape(conv_ch // HEAD, b * seq, HEAD)
    o_f, o_b, s_new = _gdr(qkv, ab, _alpha_row(a_log, nqk), _alpha_row(dt_bias, nqk), s0, b, seq)
    act = _dn_gate(o_f, o_b, proj, conv_ch, head_norm_w)
    xn = _out_proj(act, w_out, x2, seq, mods)
    return xn.reshape(b, seq, d), s_new


def _cf_layer(x, mods, norm_w, w_in, conv_w, conv_b, ln_w, ln_b, w_out, mode):
    b, seq, d = x.shape
    x2 = x.reshape(b * seq, d)
    width = w_out.shape[0]
    proj = _proj(x2, seq, mods, norm_w, w_in, 3 * width)
    if mode == "seq":
        uc = _cf_conv(proj, b, seq, conv_w, conv_b, width, 1, nb=min(4, b), tc=256)
    elif mode == "row":
        nseq = b * seq // GRID_W
        uc = _cf_conv(proj, nseq, GRID_W, conv_w, conv_b, width, 1, nb=min(16, nseq), tc=256)
    else:
        uc = _cf_conv(proj, b, seq, conv_w, conv_b, width, GRID_W, nb=1, tc=128)
    act = _cf_gate(uc, proj, ln_w, ln_b)
    xn = _out_proj(act, w_out, x2, seq, mods)
    return xn.reshape(b, seq, d)


def _to_col_major(x):
    b, seq, ch = x.shape
    rows = seq // GRID_W
    return x.reshape(b, rows, GRID_W, ch).transpose(0, 2, 1, 3).reshape(b, seq, ch)


def _to_row_major(x):
    b, seq, ch = x.shape
    rows = seq // GRID_W
    return x.reshape(b, GRID_W, rows, ch).transpose(0, 2, 1, 3).reshape(b, seq, ch)


def kernel(x_prompt, x_sample, state_l0, state_l2, c, c_ctx, ada_w, ada_b, norm_w, dn_w_in, dn_conv_w,
           dn_conv_b, dn_a_log, dn_dt_bias, dn_norm_w, dn_w_out, cf_w_in, cf_conv_w, cf_conv_b, cf_ln_w,
           cf_ln_b, cf_w_out, final_norm_w):
    depth = ada_w.shape[0]
    bs = x_sample.shape[0]
    d = x_prompt.shape[-1]
    rows = -(-(1 + bs) // SUBLANES) * SUBLANES
    cond = jnp.zeros((rows, d), F32).at[0].set(c_ctx).at[1:1 + bs].set(c)
    mods = _ada_mods(cond, ada_w, ada_b)
    caches = (state_l0, state_l2)
    new_states = []
    xp, xs = x_prompt, x_sample
    for i in range(depth):
        j = i // 2
        mp = mods[i, 0:1][:, None, :]
        ms = mods[i, 1:1 + bs][:, None, :]
        if i % 2 == 0:
            dn = (dn_w_in[j], dn_conv_w[j], dn_conv_b[j], dn_a_log[j], dn_dt_bias[j], dn_norm_w[j], dn_w_out[j])
            xp, sp = _dn_layer(xp, mp, norm_w[i], *dn, s0=None, nb_conv=min(8, xp.shape[0]))
            new_states.append(sp)
            if j % 2 == 1:
                xs_cm, _ = _dn_layer(_to_col_major(xs), ms, norm_w[i], *dn, s0=caches[j], nb_conv=1)
                xs = _to_row_major(xs_cm)
            else:
                xs, _ = _dn_layer(xs, ms, norm_w[i], *dn, s0=caches[j], nb_conv=1)
        else:
            cf = (cf_w_in[j], cf_conv_w[j], cf_conv_b[j], cf_ln_w[j], cf_ln_b[j], cf_w_out[j])
            xp = _cf_layer(xp, mp, norm_w[i], *cf, mode="seq")
            xs = _cf_layer(xs, ms, norm_w[i], *cf, mode="row" if j % 2 == 0 else "col")
    bp, sp_len, _ = xp.shape
    y_prompt = _final_norm(xp.reshape(bp * sp_len, d), final_norm_w).reshape(xp.shape)
    y_sample = _final_norm(xs.reshape(bs * xs.shape[1], d), final_norm_w).reshape(xs.shape)
    return (y_prompt, y_sample, new_states[0], new_states[1])
```

```python
import functools
import math

import jax
import jax.numpy as jnp
from jax import lax
from jax.experimental import pallas as pl
from jax.experimental.pallas import tpu as pltpu

F32 = jnp.float32
BF16 = jnp.bfloat16
EPS = 1e-6
HI = lax.Precision.HIGHEST

LANES = 128
SUBLANES = 8
VMEM_LIMIT_BYTES = 48 * 1024 * 1024

GRID_W = 64
HEAD = 128
CHUNK = 64
PAIR = 2 * CHUNK
DN_CONV_W = 5
CF_CONV_W = 31
NT_DIMS = (((1,), (1,)), ((), ()))


def _cparams(sem):
    return pltpu.CompilerParams(dimension_semantics=sem, vmem_limit_bytes=VMEM_LIMIT_BYTES)


def _silu(x):
    return x * jax.nn.sigmoid(x)


def _split_bf16(x):
    hi = x.astype(BF16)
    lo = (x - hi.astype(F32)).astype(BF16)
    return hi, lo


def _ada_kernel(c_ref, w_ref, b_ref, o_ref):
    c = c_ref[...]
    s_hi, s_lo = _split_bf16(_silu(c))
    w_hi, w_lo = _split_bf16(w_ref[...])
    acc = jnp.dot(s_hi, w_hi, preferred_element_type=F32)
    acc += jnp.dot(s_lo, w_hi, preferred_element_type=F32)
    acc += jnp.dot(s_hi, w_lo, preferred_element_type=F32)
    o_ref[...] = acc + b_ref[...]


def _ada_mods(cond, ada_w, ada_b, tn=512):
    depth, d, n = ada_w.shape
    rows = cond.shape[0]
    return pl.pallas_call(
        _ada_kernel,
        grid=(depth, n // tn),
        in_specs=[
            pl.BlockSpec((rows, d), lambda i, j: (0, 0)),
            pl.BlockSpec((None, d, tn), lambda i, j: (i, 0, j)),
            pl.BlockSpec((None, 1, tn), lambda i, j: (i, 0, j)),
        ],
        out_specs=pl.BlockSpec((None, rows, tn), lambda i, j: (i, 0, j)),
        out_shape=jax.ShapeDtypeStruct((depth, rows, n), F32),
        compiler_params=_cparams(("parallel", "parallel")),
        name="ada_mods",
    )(cond, ada_w, ada_b.reshape(depth, 1, n))


def _proj_kernel(x_ref, nw_ref, mod_ref, w_ref, *rest, d, has_tail):
    if has_tail:
        wt_ref, o_ref, ot_ref, h_ref = rest
    else:
        o_ref, h_ref = rest

    @pl.when(pl.program_id(1) == 0)
    def _():
        x = x_ref[...]
        y = x * lax.rsqrt(jnp.mean(x * x, axis=-1, keepdims=True) + EPS) * nw_ref[...]
        shift = mod_ref[0, :, 0:d]
        scale = mod_ref[0, :, d:2 * d]
        h = (y * (1.0 + scale) + shift).astype(BF16)
        h_ref[...] = h
        if has_tail:
            ot_ref[...] = jnp.dot(h, wt_ref[...].astype(BF16), preferred_element_type=F32)

    o_ref[...] = jnp.dot(h_ref[...], w_ref[...].astype(BF16), preferred_element_type=F32)


def _proj(x2, seq_len, mods, norm_w, w, n_main, w_tail=None, tm=1024, tn=512):
    t, d = x2.shape
    tm = min(tm, t)
    per_batch = mods.shape[0] > 1
    if per_batch:
        tm = math.gcd(tm, seq_len)
    mod_idx = (lambda i, j: ((i * tm) // seq_len, 0, 0)) if per_batch else (lambda i, j: (0, 0, 0))
    has_tail = w_tail is not None
    in_specs = [
        pl.BlockSpec((tm, d), lambda i, j: (i, 0)),
        pl.BlockSpec((1, d), lambda i, j: (0, 0)),
        pl.BlockSpec((1, 1, 3 * d), mod_idx),
        pl.BlockSpec((d, tn), lambda i, j: (0, j)),
    ]
    out_specs = [pl.BlockSpec((tm, tn), lambda i, j: (i, j))]
    out_shape = [jax.ShapeDtypeStruct((t, n_main), F32)]
    args = [x2, norm_w.reshape(1, d), mods, w]
    if has_tail:
        in_specs.append(pl.BlockSpec((d, LANES), lambda i, j: (0, 0)))
        out_specs.append(pl.BlockSpec((tm, LANES), lambda i, j: (i, 0)))
        out_shape.append(jax.ShapeDtypeStruct((t, LANES), F32))
        args.append(w_tail)
    outs = pl.pallas_call(
        functools.partial(_proj_kernel, d=d, has_tail=has_tail),
        grid=(t // tm, n_main // tn),
        in_specs=in_specs,
        out_specs=out_specs,
        out_shape=out_shape,
        scratch_shapes=[pltpu.VMEM((tm, d), BF16)],
        compiler_params=_cparams(("parallel", "arbitrary")),
        name="proj",
    )(*args)
    return outs if has_tail else outs[0]


def _out_kernel(a_ref, w_ref, x_ref, g_ref, o_ref):
    y = jnp.dot(a_ref[...], w_ref[...].astype(BF16), preferred_element_type=F32)
    o_ref[...] = x_ref[...] + g_ref[0] * y


def _out_proj(act, w, x2, seq_len, mods, tm=1024, tn=256):
    t, d = x2.shape
    k = act.shape[1]
    tm = min(tm, t)
    per_batch = mods.shape[0] > 1
    if per_batch:
        tm = math.gcd(tm, seq_len)
    gate_blk = 2 * d // tn
    mod_idx = ((lambda i, j: ((i * tm) // seq_len, 0, gate_blk + j)) if per_batch
               else (lambda i, j: (0, 0, gate_blk + j)))
    return pl.pallas_call(
        _out_kernel,
        grid=(t // tm, d // tn),
        in_specs=[
            pl.BlockSpec((tm, k), lambda i, j: (i, 0)),
            pl.BlockSpec((k, tn), lambda i, j: (0, j)),
            pl.BlockSpec((tm, tn), lambda i, j: (i, j)),
            pl.BlockSpec((1, 1, tn), mod_idx),
        ],
        out_specs=pl.BlockSpec((tm, tn), lambda i, j: (i, j)),
        out_shape=jax.ShapeDtypeStruct((t, d), F32),
        compiler_params=_cparams(("parallel", "arbitrary")),
        name="out_proj",
    )(act, w, x2, mods)


def _conv_rows(pad_ref, w_ref, r0, rb, taps, dil, pad_al):
    half = (taps - 1) // 2
    base = pad_al - half * dil
    acc = None
    if dil % SUBLANES == 0 and base % SUBLANES == 0:
        for k in range(taps):
            v = pad_ref[pl.ds(pl.multiple_of(r0 + base + k * dil, SUBLANES), rb), :]
            term = v * w_ref[k:k + 1, :]
            acc = term if acc is None else acc + term
    else:
        lo = (base // SUBLANES) * SUBLANES
        span = -(-(base - lo + (taps - 1) * dil + rb) // SUBLANES) * SUBLANES
        groups = {}
        for k in range(taps):
            off = base - lo + k * dil
            groups.setdefault(off % SUBLANES, []).append((k, off // SUBLANES))
        accs = []
        for lt in range(pad_ref.shape[1] // LANES):
            cols = slice(lt * LANES, (lt + 1) * LANES)
            win = pad_ref[pl.ds(pl.multiple_of(r0 + lo, SUBLANES), span), cols]
            acc = None
            for res, members in groups.items():
                rows = max(a for _, a in members) * SUBLANES + rb
                shifted = win[res:res + rows, :]
                for k, a in members:
                    term = shifted[a * SUBLANES:a * SUBLANES + rb, :] * w_ref[k:k + 1, cols]
                    acc = term if acc is None else acc + term
            accs.append(acc)
        acc = accs[0] if len(accs) == 1 else jnp.concatenate(accs, axis=1)
    return acc


def _dnconv_kernel(x_ref, w_ref, b_ref, o_ref, pad_ref, *, nb, seq, rb, pad_al, n_norm_blocks):
    tc = pad_ref.shape[1]
    zeros = jnp.zeros((pad_al, tc), F32)
    pad_ref[0:pad_al, :] = zeros
    pad_ref[pad_al + seq:pad_al + seq + pad_al, :] = zeros
    is_qk = pl.program_id(1) < n_norm_blocks
    bias = b_ref[...]

    def seq_body(s, carry):
        pad_ref[pad_al:pad_al + seq, :] = x_ref[s]

        def chunk_body(ci, carry2):
            r0 = pl.multiple_of(ci * rb, rb)
            acc = _conv_rows(pad_ref, w_ref, r0, rb, DN_CONV_W, 1, pad_al)
            y = _silu(acc + bias)
            for g in range(tc // HEAD):
                yg = y[:, g * HEAD:(g + 1) * HEAD]
                nrm = yg * lax.rsqrt(jnp.sum(yg * yg, axis=-1, keepdims=True) + EPS)
                o_ref[g, s, pl.ds(r0, rb), :] = jnp.where(is_qk, nrm, yg)
            return carry2

        lax.fori_loop(0, seq // rb, chunk_body, 0)
        return carry

    lax.fori_loop(0, nb, seq_body, 0)


def _dn_conv(proj, nseq, seq, conv_w, conv_b, n_ch, n_norm, nb, tc=256, rb=64):
    width = proj.shape[1]
    x3 = proj.reshape(nseq, seq, width)
    pad_al = SUBLANES
    wpad = jnp.zeros((SUBLANES, n_ch), F32).at[:DN_CONV_W].set(conv_w)
    rb = min(rb, seq)
    return pl.pallas_call(
        functools.partial(_dnconv_kernel, nb=nb, seq=seq, rb=rb, pad_al=pad_al,
                          n_norm_blocks=n_norm // tc),
        grid=(nseq // nb, n_ch // tc),
        in_specs=[
            pl.BlockSpec((nb, seq, tc), lambda s, j: (s, 0, j)),
            pl.BlockSpec((SUBLANES, tc), lambda s, j: (0, j)),
            pl.BlockSpec((1, tc), lambda s, j: (0, j)),
        ],
        out_specs=pl.BlockSpec((tc // HEAD, nb, seq, HEAD), lambda s, j: (j, s, 0, 0)),
        out_shape=jax.ShapeDtypeStruct((n_ch // HEAD, nseq, seq, HEAD), F32),
        scratch_shapes=[pltpu.VMEM((seq + 2 * pad_al, tc), F32)],
        compiler_params=_cparams(("parallel", "parallel")),
        name="dn_conv",
    )(x3, wpad, conv_b.reshape(1, n_ch))


def _cfconv_kernel(a_ref, b_ref, w_ref, bias_ref, o_ref, pad_ref, *, nb, seq, rb, dil, pad_al):
    tc = pad_ref.shape[1]
    zeros = jnp.zeros((pad_al, tc), F32)
    pad_ref[0:pad_al, :] = zeros
    pad_ref[pad_al + seq:pad_al + seq + pad_al, :] = zeros
    bias = bias_ref[...]

    def seq_body(s, carry):
        a = a_ref[s]
        pad_ref[pad_al:pad_al + seq, :] = a * jax.nn.sigmoid(b_ref[s])

        def chunk_body(ci, carry2):
            r0 = pl.multiple_of(ci * rb, rb)
            acc = _conv_rows(pad_ref, w_ref, r0, rb, CF_CONV_W, dil, pad_al)
            o_ref[s, pl.ds(r0, rb), :] = acc + bias
            return carry2

        lax.fori_loop(0, seq // rb, chunk_body, 0)
        return carry

    lax.fori_loop(0, nb, seq_body, 0)


def _cf_conv(proj, nseq, seq, conv_w, conv_b, width, dil, nb, tc, rb=64):
    x3 = proj.reshape(nseq, seq, proj.shape[1])
    half = (CF_CONV_W - 1) // 2
    pad_al = -(-(half * dil) // SUBLANES) * SUBLANES
    wrows = -(-CF_CONV_W // SUBLANES) * SUBLANES
    wpad = jnp.zeros((wrows, width), F32).at[:CF_CONV_W].set(conv_w)
    rb = min(rb, seq)
    nblk = width // tc
    out = pl.pallas_call(
        functools.partial(_cfconv_kernel, nb=nb, seq=seq, rb=rb, dil=dil, pad_al=pad_al),
        grid=(nseq // nb, nblk),
        in_specs=[
            pl.BlockSpec((nb, seq, tc), lambda s, j: (s, 0, j)),
            pl.BlockSpec((nb, seq, tc), lambda s, j: (s, 0, nblk + j)),
            pl.BlockSpec((wrows, tc), lambda s, j: (0, j)),
            pl.BlockSpec((1, tc), lambda s, j: (0, j)),
        ],
        out_specs=pl.BlockSpec((nb, seq, tc), lambda s, j: (s, 0, j)),
        out_shape=jax.ShapeDtypeStruct((nseq, seq, width), F32),
        scratch_shapes=[pltpu.VMEM((seq + 2 * pad_al, tc), F32)],
        compiler_params=_cparams(("parallel", "parallel")),
        name="cf_conv",
    )(x3, x3, wpad, conv_b.reshape(1, width))
    return out.reshape(nseq * seq, width)


def _cfgate_kernel(u_ref, z_ref, w_ref, b_ref, o_ref):
    u = u_ref[...]
    mu = jnp.mean(u, axis=-1, keepdims=True)
    uc = u - mu
    var = jnp.mean(uc * uc, axis=-1, keepdims=True)
    y = uc * lax.rsqrt(var + EPS) * w_ref[...] + b_ref[...]
    o_ref[...] = (_silu(y) * _silu(z_ref[...])).astype(BF16)


def _cf_gate(uc, proj, ln_w, ln_b, tm=256):
    t, width = uc.shape
    tm = min(tm, t)
    return pl.pallas_call(
        _cfgate_kernel,
        grid=(t // tm,),
        in_specs=[
            pl.BlockSpec((tm, width), lambda i: (i, 0)),
            pl.BlockSpec((tm, width), lambda i: (i, 2)),
            pl.BlockSpec((1, width), lambda i: (0, 0)),
            pl.BlockSpec((1, width), lambda i: (0, 0)),
        ],
        out_specs=pl.BlockSpec((tm, width), lambda i: (i, 0)),
        out_shape=jax.ShapeDtypeStruct((t, width), BF16),
        compiler_params=_cparams(("parallel",)),
        name="cf_gate",
    )(uc, proj, ln_w.reshape(1, width), ln_b.reshape(1, width))


def _dngate_kernel(of_ref, ob_ref, z_ref, w_ref, o_ref, *, hb):
    for h in range(hb):
        o = of_ref[h] + ob_ref[h]
        o = o * lax.rsqrt(jnp.mean(o * o, axis=-1, keepdims=True) + EPS) * w_ref[...]
        z = z_ref[:, h * HEAD:(h + 1) * HEAD]
        o_ref[:, h * HEAD:(h + 1) * HEAD] = (o * _silu(z)).astype(BF16)


def _dn_gate(o_f, o_b, proj, z_col0, norm_w, tm=512, hb=4):
    nh, t, _ = o_f.shape
    tm = min(tm, t)
    zblk = z_col0 // (hb * HEAD)
    return pl.pallas_call(
        functools.partial(_dngate_kernel, hb=hb),
        grid=(t // tm, nh // hb),
        in_specs=[
            pl.BlockSpec((hb, tm, HEAD), lambda i, j: (j, i, 0)),
            pl.BlockSpec((hb, tm, HEAD), lambda i, j: (j, i, 0)),
            pl.BlockSpec((tm, hb * HEAD), lambda i, j: (i, zblk + j)),
            pl.BlockSpec((1, HEAD), lambda i, j: (0, 0)),
        ],
        out_specs=pl.BlockSpec((tm, hb * HEAD), lambda i, j: (i, j)),
        out_shape=jax.ShapeDtypeStruct((t, nh * HEAD), BF16),
        compiler_params=_cparams(("parallel", "parallel")),
        name="dn_gate",
    )(o_f, o_b, proj, norm_w.reshape(1, HEAD))


def _block_diag(a, b):
    za = jnp.zeros((a.shape[0], b.shape[1]), a.dtype)
    zb = jnp.zeros((b.shape[0], a.shape[1]), a.dtype)
    return jnp.concatenate([jnp.concatenate([a, za], axis=1), jnp.concatenate([zb, b], axis=1)], axis=0)


def _log_decay_and_beta(ab, alog_row, dtb_row):
    x = ab + dtb_row
    softplus = jnp.maximum(x, 0.0) + jnp.log1p(jnp.exp(-jnp.abs(x)))
    return -jnp.exp(alog_row) * softplus, jax.nn.sigmoid(ab)


def _chunk_cumsum(g_all):
    row = lax.broadcasted_iota(jnp.int32, (CHUNK, CHUNK), 0)
    col = lax.broadcasted_iota(jnp.int32, (CHUNK, CHUNK), 1)
    lane = lax.broadcasted_iota(jnp.int32, (CHUNK, LANES), 1)
    gam_f = jnp.dot((row >= col).astype(F32), g_all, precision=HI, preferred_element_type=F32)
    gam_b = jnp.dot((row <= col).astype(F32), g_all, precision=HI, preferred_element_type=F32)
    return jnp.where(lane < LANES // 2, gam_f, gam_b)


def _gdr_inv_kernel(k_ref, ab_ref, alog_ref, dtb_ref, z_ref, gs_s, gl_s, tl_s, a_s, beta_s, gam_s, *, nqk, ncs):
    lane = lax.broadcasted_iota(jnp.int32, (CHUNK, LANES), 1)
    sub8 = lax.broadcasted_iota(jnp.int32, (SUBLANES, LANES), 0)
    half = LANES // 2

    beta_l = [jnp.zeros((CHUNK, LANES), F32)] * 2
    gam_l = [jnp.zeros((CHUNK, LANES), F32)] * 2
    for c in range(ncs):
        g_all, beta_all = _log_decay_and_beta(ab_ref[c * CHUNK:(c + 1) * CHUNK, :], alog_ref[...], dtb_ref[...])
        gam = _chunk_cumsum(g_all)
        for r in range(2):
            dst = r * half + c * nqk
            sel = lax.shift_right_logical(lane, 4) == dst // nqk
            for d in range(2):
                src_b = d * half + r * nqk
                src_g = src_b + 2 * nqk
                beta_l[d] = jnp.where(sel, pltpu.roll(beta_all, (dst - src_b) % LANES, 1), beta_l[d])
                gam_l[d] = jnp.where(sel, pltpu.roll(gam, (dst - src_g) % LANES, 1), gam_l[d])
    for d in range(2):
        beta_s[d] = beta_l[d]
        gam_s[d] = gam_l[d]

    def gram_body(c, carry):
        r0 = pl.multiple_of(c * CHUNK, CHUNK)
        for hp in range(nqk // 2):
            ka = k_ref[2 * hp, pl.ds(r0, CHUNK), :].astype(BF16)
            kb = k_ref[2 * hp + 1, pl.ds(r0, CHUNK), :].astype(BF16)
            gg = lax.dot_general(jnp.concatenate([ka, kb], axis=0), jnp.concatenate([ka, ka, kb, kb], axis=0),
                                 NT_DIMS, preferred_element_type=F32)
            for e in range(2):
                g_dup = gg[e * CHUNK:(e + 1) * CHUNK, e * PAIR:(e + 1) * PAIR]
                g_pk = jnp.where(lane < CHUNK, g_dup, pltpu.roll(g_dup, CHUNK - 1, 0))
                row0 = pl.multiple_of((c * nqk + 2 * hp + e) * CHUNK, CHUNK)
                gs_s[pl.ds(row0, CHUNK), :] = g_pk
        return carry

    lax.fori_loop(0, ncs, gram_body, 0)

    def gt_body(ip, carry):
        m = gs_s[pl.ds(2 * ip, half, stride=CHUNK), :]
        mt = jnp.concatenate([m, m], axis=0).T
        gl_s[2 * ip] = mt[:CHUNK]
        gl_s[2 * ip + 1] = mt[CHUNK:]
        return carry

    lax.fori_loop(0, CHUNK // 2, gt_body, 0)

    sq_row = lax.broadcasted_iota(jnp.int32, (LANES, LANES), 0)
    sq_col = lax.broadcasted_iota(jnp.int32, (LANES, LANES), 1)
    keep = lax.shift_right_logical(sq_col, 6) == lax.shift_right_logical(sq_row, 6)
    eye_b = (sq_row == sq_col).astype(BF16)
    rows_per_dot = 4
    nblk = CHUNK // SUBLANES

    for d in range(2):
        fwd = d == 0
        for i in (range(CHUNK) if fwd else range(CHUNK - 1, -1, -1)):
            before = range(0, i) if fwd else range(i + 1, CHUNK)
            ib = i // SUBLANES
            own_blocks = range(0, ib + 1) if fwd else range(ib, nblk)
            acc = {cb: ((sub8 + cb * SUBLANES) == i).astype(F32) if cb == ib else jnp.zeros((SUBLANES, LANES), F32)
                   for cb in own_blocks}
            if len(before):
                g_i = gam_s[d, i:i + 1, :]
                b_i = beta_s[d, i:i + 1, :]
                for jb in sorted({j // SUBLANES for j in before}):
                    js = sub8 + jb * SUBLANES
                    valid = (js < i) if fwd else (js > i)
                    g_j = gam_s[d, jb * SUBLANES:(jb + 1) * SUBLANES, :]
                    dec = jnp.exp(jnp.where(valid, g_i - g_j, -jnp.inf))
                    a_s[jb * SUBLANES:(jb + 1) * SUBLANES, :] = (
                        gl_s[i, jb * SUBLANES:(jb + 1) * SUBLANES, :] * (b_i * dec))
                for j in before:
                    a_ij = a_s[j:j + 1, :]
                    jbk = j // SUBLANES
                    for cb in (range(0, jbk + 1) if fwd else range(jbk, nblk)):
                        acc[cb] = acc[cb] - a_ij * tl_s[j, cb * SUBLANES:(cb + 1) * SUBLANES, :]
            for cb in range(nblk):
                tl_s[i, cb * SUBLANES:(cb + 1) * SUBLANES, :] = acc.get(cb, jnp.zeros((SUBLANES, LANES), F32))

        def tb_body(ig, carry):
            slabs = []
            for t in range(rows_per_dot):
                slab = tl_s[ig * rows_per_dot + t].astype(BF16)
                slabs += [slab, slab]
            st_all = lax.dot_general(eye_b, jnp.concatenate(slabs, axis=0), NT_DIMS,
                                     preferred_element_type=F32)
            for t in range(rows_per_dot):
                st = jnp.where(keep, st_all[:, t * LANES:(t + 1) * LANES], 0.0)
                for c in range(ncs):
                    for r in range(2):
                        z_ref[d, 0, c, ig * rows_per_dot + t, r] = st[r * half + c * nqk:r * half + (c + 1) * nqk]
            return carry

        lax.fori_loop(0, CHUNK // rows_per_dot, tb_body, 0)


def _gdr_inv(qkv, ab, alog_row, dtb_row):
    ng, t, _ = qkv.shape
    nqk = ng // 4
    ncs = LANES // (2 * nqk)
    nc = t // CHUNK
    rows = ncs * CHUNK
    z = pl.pallas_call(
        functools.partial(_gdr_inv_kernel, nqk=nqk, ncs=ncs),
        grid=(nc // ncs,),
        in_specs=[
            pl.BlockSpec((nqk, rows, HEAD), lambda s: (1, s, 0)),
            pl.BlockSpec((rows, LANES), lambda s: (s, 0)),
            pl.BlockSpec((1, LANES), lambda s: (0, 0)),
            pl.BlockSpec((1, LANES), lambda s: (0, 0)),
        ],
        out_specs=pl.BlockSpec((2, 1, ncs, CHUNK, 2, nqk, LANES), lambda s: (0, s, 0, 0, 0, 0, 0)),
        out_shape=jax.ShapeDtypeStruct((2, nc // ncs, ncs, CHUNK, 2, nqk, LANES), F32),
        scratch_shapes=[
            pltpu.VMEM((ncs * nqk * CHUNK, LANES), F32),
            pltpu.VMEM((CHUNK, CHUNK, LANES), F32),
            pltpu.VMEM((CHUNK, CHUNK, LANES), F32),
            pltpu.VMEM((CHUNK, LANES), F32),
            pltpu.VMEM((2, CHUNK, LANES), F32),
            pltpu.VMEM((2, CHUNK, LANES), F32),
        ],
        compiler_params=_cparams(("parallel",)),
        name="gdr_inv",
    )(qkv, ab, alog_row, dtb_row)
    return z.reshape(2, nc, CHUNK * 2 * nqk, LANES)


def _gdr_prep_kernel(q_ref, k_ref, v_ref, ab_ref, t_ref, alog_ref, dtb_ref, wq_ref, u_ref, ak_ref,
                     gam_s, beta_s, gamt_s, *, hqb, nqk):
    nv = 2 * nqk
    prow = lax.broadcasted_iota(jnp.int32, (PAIR, PAIR), 0)
    pcol = lax.broadcasted_iota(jnp.int32, (PAIR, PAIR), 1)
    same_prob = lax.shift_right_logical(prow, 6) == lax.shift_right_logical(pcol, 6)
    incl = (same_prob & (prow >= pcol), same_prob & (prow <= pcol))
    eye_l = (prow == pcol).astype(F32)
    lane1 = lax.broadcasted_iota(jnp.int32, (1, PAIR), 1)
    scale = HEAD ** -0.5

    g_all, beta_all = _log_decay_and_beta(ab_ref[...], alog_ref[...], dtb_ref[...])
    gam = _chunk_cumsum(g_all)
    gam_s[...] = gam
    beta_s[...] = beta_all
    gam2 = jnp.concatenate([gam, pltpu.roll(gam, LANES - nqk, 1)], axis=0)
    gamt_s[...] = lax.dot_general(eye_l, gam2, NT_DIMS, precision=HI, preferred_element_type=F32)

    def block_body(blk, carry):
        shift = (LANES - blk * hqb) % LANES
        gam_r = pltpu.roll(gam_s[...], shift, 1)
        beta_r = pltpu.roll(beta_s[...], shift, 1)
        gam_t = [gamt_s[pl.ds(pl.multiple_of(d * 2 * nv + nv + blk * hqb, hqb), hqb), :] for d in range(2)]
        for i in range(hqb):
            hq = blk * hqb + i
            k = k_ref[hq]
            qs = q_ref[hq] * scale
            k2 = jnp.concatenate([k, k], axis=0)
            gq = lax.dot_general(jnp.concatenate([qs.astype(BF16), eye_l.astype(BF16)], axis=0), k2.astype(BF16),
                                 NT_DIMS, preferred_element_type=F32)
            qk2, kt2 = gq[:CHUNK], gq[CHUNK:]
            q2 = jnp.concatenate([qs, qs], axis=0)
            v2 = jnp.concatenate([v_ref[2 * hq], v_ref[2 * hq + 1]], axis=0)
            rhs, qg, glrow = [], [], []
            for d in range(2):
                lane_b = d * 2 * nv + i
                lane_g = lane_b + nv
                cg = jnp.concatenate([gam_r[:, lane_g:lane_g + 1], gam_r[:, lane_g + nqk:lane_g + nqk + 1]], axis=0)
                cb = jnp.concatenate([beta_r[:, lane_b:lane_b + 1], beta_r[:, lane_b + nqk:lane_b + nqk + 1]], axis=0)
                rg = gam_t[d][i:i + 1, :]
                last = CHUNK - 1 if d == 0 else 0
                gl = jnp.where(lane1 < CHUNK, rg[:, last:last + 1], rg[:, CHUNK + last:CHUNK + last + 1])
                cg = jnp.broadcast_to(cg, (PAIR, HEAD))
                cb = jnp.broadcast_to(cb, (PAIR, HEAD))
                dec = jnp.exp(jnp.where(incl[d], cg - rg, -jnp.inf))
                eg = jnp.exp(cg)
                rhs.append(jnp.concatenate([v2 * cb, k2 * (cb * eg)], axis=1).astype(BF16))
                qg.append(q2 * eg)
                glrow.append(gl)
                dsum = dec[:CHUNK] + dec[CHUNK:]
                ak_ref[d, 0, hq] = jnp.concatenate([qk2 * dsum, kt2 * jnp.exp(gl - rg)], axis=0).astype(BF16)
            t_pair = [jnp.concatenate([t_ref[d, 0, pl.ds(hq, CHUNK, stride=nv), :],
                                       t_ref[d, 0, pl.ds(nqk + hq, CHUNK, stride=nv), :]], axis=0).astype(BF16)
                      for d in range(2)]
            tq = _block_diag(t_pair[0], t_pair[1])
            uw = jnp.dot(tq, jnp.concatenate(rhs, axis=0), preferred_element_type=F32)
            for d in range(2):
                uw_d = uw[d * PAIR:(d + 1) * PAIR]
                u2, w2 = uw_d[:, :HEAD], uw_d[:, HEAD:]
                wq_a = jnp.concatenate([w2[:CHUNK], qg[d][:CHUNK]], axis=0)
                wq_b = jnp.concatenate([w2[CHUNK:], qg[d][CHUNK:]], axis=0)
                wq_ref[d, 0, hq] = jnp.concatenate([wq_a, wq_b], axis=1).astype(BF16)
                u_ref[d, 0, hq, 0:PAIR, :] = u2
                egl = jnp.exp(glrow[d])
                u_ref[d, 0, hq, PAIR:PAIR + SUBLANES, :] = jnp.broadcast_to(egl[:, 0:1], (SUBLANES, HEAD))
                u_ref[d, 0, hq, PAIR + SUBLANES:PAIR + 2 * SUBLANES, :] = jnp.broadcast_to(
                    egl[:, CHUNK:CHUNK + 1], (SUBLANES, HEAD))
        return carry

    lax.fori_loop(0, nqk // hqb, block_body, 0)


def _gdr_prep(qkv, ab, t_inv, alog_row, dtb_row, hqb=8):
    ng, t, _ = qkv.shape
    nqk = ng // 4
    nv = 2 * nqk
    nc = t // CHUNK
    urows = PAIR + 2 * SUBLANES
    return pl.pallas_call(
        functools.partial(_gdr_prep_kernel, hqb=hqb, nqk=nqk),
        grid=(nc,),
        in_specs=[
            pl.BlockSpec((nqk, CHUNK, HEAD), lambda c: (0, c, 0)),
            pl.BlockSpec((nqk, CHUNK, HEAD), lambda c: (1, c, 0)),
            pl.BlockSpec((nv, CHUNK, HEAD), lambda c: (1, c, 0)),
            pl.BlockSpec((CHUNK, LANES), lambda c: (c, 0)),
            pl.BlockSpec((2, 1, CHUNK * nv, LANES), lambda c: (0, c, 0, 0)),
            pl.BlockSpec((1, LANES), lambda c: (0, 0)),
            pl.BlockSpec((1, LANES), lambda c: (0, 0)),
        ],
        out_specs=[
            pl.BlockSpec((2, 1, nqk, PAIR, 2 * HEAD), lambda c: (0, c, 0, 0, 0)),
            pl.BlockSpec((2, 1, nqk, urows, HEAD), lambda c: (0, c, 0, 0, 0)),
            pl.BlockSpec((2, 1, nqk, CHUNK + HEAD, PAIR), lambda c: (0, c, 0, 0, 0)),
        ],
        out_shape=[
            jax.ShapeDtypeStruct((2, nc, nqk, PAIR, 2 * HEAD), BF16),
            jax.ShapeDtypeStruct((2, nc, nqk, urows, HEAD), F32),
            jax.ShapeDtypeStruct((2, nc, nqk, CHUNK + HEAD, PAIR), BF16),
        ],
        scratch_shapes=[pltpu.VMEM((CHUNK, LANES), F32), pltpu.VMEM((CHUNK, LANES), F32),
                        pltpu.VMEM((LANES, PAIR), F32)],
        compiler_params=_cparams(("parallel",)),
        name="gdr_prep",
    )(qkv, qkv, qkv, ab, t_inv, alog_row, dtb_row)


def _gdr_scan_kernel(*refs, nqk, n_chunks, has_s0):
    wqf_ref, uf_ref, akf_ref, wqb_ref, ub_ref, akb_ref = refs[:6]
    if has_s0:
        s0_ref = refs[6]
        of_ref, ob_ref, sout_ref, s_scr = refs[7:]
    else:
        of_ref, ob_ref, sout_ref, s_scr = refs[6:]
    n = pl.program_id(1)

    @pl.when(n == 0)
    def _():
        if has_s0:
            s_scr[...] = s0_ref[0]
        else:
            s_scr[...] = jnp.zeros(s_scr.shape, F32)

    dirs = ((wqf_ref, uf_ref, of_ref), (wqb_ref, ub_ref, ob_ref))
    zero_v = jnp.zeros((CHUNK, HEAD), BF16)
    for hq in range(nqk):
        m1s, v_rows = [], []
        for d, (wq_ref, u_ref, _) in enumerate(dirs):
            s_a = s_scr[d, 2 * hq]
            s_b = s_scr[d, 2 * hq + 1]
            m1 = jnp.dot(wq_ref[0, 0, hq], _block_diag(s_a.astype(BF16), s_b.astype(BF16)),
                         preferred_element_type=F32)
            m1s.append(m1)
            for r in range(2):
                p = 2 * d + r
                v_new = (u_ref[0, 0, hq, r * CHUNK:(r + 1) * CHUNK, :]
                         - m1[:CHUNK, r * HEAD:(r + 1) * HEAD]).astype(BF16)
                v_rows.append(jnp.concatenate([v_new if c == p else zero_v for c in range(4)], axis=1))
        ak = jnp.concatenate([akf_ref[0, 0, hq], akb_ref[0, 0, hq]], axis=1)
        m2 = jnp.dot(ak, jnp.concatenate(v_rows, axis=0), preferred_element_type=F32)
        for d, (_, u_ref, o_ref) in enumerate(dirs):
            for r in range(2):
                p = 2 * d + r
                hv = 2 * hq + r
                m2_p = m2[:, p * HEAD:(p + 1) * HEAD]
                o_ref[hv] = m1s[d][CHUNK:, r * HEAD:(r + 1) * HEAD] + m2_p[:CHUNK]
                decay = u_ref[0, 0, hq, PAIR + r * SUBLANES:PAIR + r * SUBLANES + 1, :]
                s_scr[d, hv] = s_scr[d, hv] * decay + m2_p[CHUNK:]

    @pl.when(n == n_chunks - 1)
    def _():
        sout_ref[0] = s_scr[...]


def _gdr_scan(wq, u, ak, s0, nseq, seq):
    _, nc, nqk, _, _ = wq.shape
    nv = 2 * nqk
    n = seq // CHUNK
    t = nc * CHUNK
    urows = u.shape[3]
    has_s0 = s0 is not None
    fwd = lambda s, c: s * n + c
    bwd = lambda s, c: s * n + (n - 1 - c)

    def specs(d, cb):
        return [
            pl.BlockSpec((1, 1, nqk, PAIR, 2 * HEAD), lambda s, c: (d, cb(s, c), 0, 0, 0)),
            pl.BlockSpec((1, 1, nqk, urows, HEAD), lambda s, c: (d, cb(s, c), 0, 0, 0)),
            pl.BlockSpec((1, 1, nqk, CHUNK + HEAD, PAIR), lambda s, c: (d, cb(s, c), 0, 0, 0)),
        ]

    in_specs = specs(0, fwd) + specs(1, bwd)
    args = [wq, u, ak, wq, u, ak]
    if has_s0:
        in_specs.append(pl.BlockSpec((1, 2, nv, HEAD, HEAD), lambda s, c: (s, 0, 0, 0, 0)))
        args.append(s0)
    o_f, o_b, s_new = pl.pallas_call(
        functools.partial(_gdr_scan_kernel, nqk=nqk, n_chunks=n, has_s0=has_s0),
        grid=(nseq, n),
        in_specs=in_specs,
        out_specs=[
            pl.BlockSpec((nv, CHUNK, HEAD), lambda s, c: (0, fwd(s, c), 0)),
            pl.BlockSpec((nv, CHUNK, HEAD), lambda s, c: (0, bwd(s, c), 0)),
            pl.BlockSpec((1, 2, nv, HEAD, HEAD), lambda s, c: (s, 0, 0, 0, 0)),
        ],
        out_shape=[
            jax.ShapeDtypeStruct((nv, t, HEAD), F32),
            jax.ShapeDtypeStruct((nv, t, HEAD), F32),
            jax.ShapeDtypeStruct((nseq, 2, nv, HEAD, HEAD), F32),
        ],
        scratch_shapes=[pltpu.VMEM((2, nv, HEAD, HEAD), F32)],
        compiler_params=_cparams(("parallel", "arbitrary")),
        name="gdr_scan",
    )(*args)
    return o_f, o_b, s_new


def _gdr(qkv, ab, alog_row, dtb_row, s0, nseq, seq):
    t_inv = _gdr_inv(qkv, ab, alog_row, dtb_row)
    wq, u, ak = _gdr_prep(qkv, ab, t_inv, alog_row, dtb_row)
    return _gdr_scan(wq, u, ak, s0, nseq, seq)


def _rms_kernel(x_ref, w_ref, o_ref):
    x = x_ref[...]
    o_ref[...] = x * lax.rsqrt(jnp.mean(x * x, axis=-1, keepdims=True) + EPS) * w_ref[...]


def _final_norm(x2, w, tm=512):
    t, d = x2.shape
    tm = min(tm, t)
    return pl.pallas_call(
        _rms_kernel,
        grid=(t // tm,),
        in_specs=[pl.BlockSpec((tm, d), lambda i: (i, 0)), pl.BlockSpec((1, d), lambda i: (0, 0))],
        out_specs=pl.BlockSpec((tm, d), lambda i: (i, 0)),
        out_shape=jax.ShapeDtypeStruct((t, d), F32),
        compiler_params=_cparams(("parallel",)),
        name="final_norm",
    )(x2, w.reshape(1, d))


def _permute_ab_cols(w_tail, nqk):
    lead = w_tail.shape[:-1]
    w = w_tail.reshape(lead + (2, 2, nqk, 2))
    return jnp.swapaxes(w, -1, -2).reshape(lead + (8 * nqk,))


def _alpha_row(p, nqk):
    full = jnp.stack([jnp.zeros_like(p), p], axis=1)
    return _permute_ab_cols(full.reshape(1, -1), nqk)


def _dn_layer(x, mods, norm_w, w_in, conv_w, conv_b, a_log, dt_bias, head_norm_w, w_out, s0, nb_conv):
    b, seq, d = x.shape
    x2 = x.reshape(b * seq, d)
    val_dim = w_out.shape[0]
    nv = val_dim // HEAD
    nqk = nv // 2
    key_dim = nqk * HEAD
    conv_ch = 2 * key_dim + val_dim
    n_main = conv_ch + val_dim
    w_tail = _permute_ab_cols(w_in[:, n_main:], nqk)
    proj, ab = _proj(x2, seq, mods, norm_w, w_in, n_main, w_tail)
    qkv = _dn_conv(proj, b, seq, conv_w, conv_b, conv_ch, 2 * key_dim, nb_conv)
    qkv = qkv.reshape(conv_ch // HEAD, b * seq, HEAD)
    o_f, o_b, s_new = _gdr(qkv, ab, _alpha_row(a_log, nqk), _alpha_row(dt_bias, nqk), s0, b, seq)
    act = _dn_gate(o_f, o_b, proj, conv_ch, head_norm_w)
    xn = _out_proj(act, w_out, x2, seq, mods)
    return xn.reshape(b, seq, d), s_new


def _cf_layer(x, mods, norm_w, w_in, conv_w, conv_b, ln_w, ln_b, w_out, mode):
    b, seq, d = x.shape
    x2 = x.reshape(b * seq, d)
    width = w_out.shape[0]
    proj = _proj(x2, seq, mods, norm_w, w_in, 3 * width)
    if mode == "seq":
        uc = _cf_conv(proj, b, seq, conv_w, conv_b, width, 1, nb=min(4, b), tc=256)
    elif mode == "row":
        nseq = b * seq // GRID_W
        uc = _cf_conv(proj, nseq, GRID_W, conv_w, conv_b, width, 1, nb=min(16, nseq), tc=256)
    else:
        uc = _cf_conv(proj, b, seq, conv_w, conv_b, width, GRID_W, nb=1, tc=128)
    act = _cf_gate(uc, proj, ln_w, ln_b)
    xn = _out_proj(act, w_out, x2, seq, mods)
    return xn.reshape(b, seq, d)


def _to_col_major(x):
    b, seq, ch = x.shape
    rows = seq // GRID_W
    return x.reshape(b, rows, GRID_W, ch).transpose(0, 2, 1, 3).reshape(b, seq, ch)


def _to_row_major(x):
    b, seq, ch = x.shape
    rows = seq // GRID_W
    return x.reshape(b, GRID_W, rows, ch).transpose(0, 2, 1, 3).reshape(b, seq, ch)


def kernel(x_prompt, x_sample, state_l0, state_l2, c, c_ctx, ada_w, ada_b, norm_w, dn_w_in, dn_conv_w,
           dn_conv_b, dn_a_log, dn_dt_bias, dn_norm_w, dn_w_out, cf_w_in, cf_conv_w, cf_conv_b, cf_ln_w,
           cf_ln_b, cf_w_out, final_norm_w):
    depth = ada_w.shape[0]
    bs = x_sample.shape[0]
    d = x_prompt.shape[-1]
    rows = -(-(1 + bs) // SUBLANES) * SUBLANES
    cond = jnp.zeros((rows, d), F32).at[0].set(c_ctx).at[1:1 + bs].set(c)
    mods = _ada_mods(cond, ada_w, ada_b)
    caches = (state_l0, state_l2)
    new_states = []
    xp, xs = x_prompt, x_sample
    for i in range(depth):
        j = i // 2
        mp = mods[i, 0:1][:, None, :]
        ms = mods[i, 1:1 + bs][:, None, :]
        if i % 2 == 0:
            dn = (dn_w_in[j], dn_conv_w[j], dn_conv_b[j], dn_a_log[j], dn_dt_bias[j], dn_norm_w[j], dn_w_out[j])
            xp, sp = _dn_layer(xp, mp, norm_w[i], *dn, s0=None, nb_conv=min(8, xp.shape[0]))
            new_states.append(sp)
            if j % 2 == 1:
                xs_cm, _ = _dn_layer(_to_col_major(xs), ms, norm_w[i], *dn, s0=caches[j], nb_conv=1)
                xs = _to_row_major(xs_cm)
            else:
                xs, _ = _dn_layer(xs, ms, norm_w[i], *dn, s0=caches[j], nb_conv=1)
        else:
            cf = (cf_w_in[j], cf_conv_w[j], cf_conv_b[j], cf_ln_w[j], cf_ln_b[j], cf_w_out[j])
            xp = _cf_layer(xp, mp, norm_w[i], *cf, mode="seq")
            xs = _cf_layer(xs, ms, norm_w[i], *cf, mode="row" if j % 2 == 0 else "col")
    bp, sp_len, _ = xp.shape
    y_prompt = _final_norm(xp.reshape(bp * sp_len, d), final_norm_w).reshape(xp.shape)
    y_sample = _final_norm(xs.reshape(bs * xs.shape[1], d), final_norm_w).reshape(xs.shape)
    return (y_prompt, y_sample, new_states[0], new_states[1])
```

```python
import functools
import math

import jax
import jax.numpy as jnp
from jax import lax
from jax.experimental import pallas as pl
from jax.experimental.pallas import tpu as pltpu

F32 = jnp.float32
BF16 = jnp.bfloat16
EPS = 1e-6
HI = lax.Precision.HIGHEST

LANES = 128
SUBLANES = 8
VMEM_LIMIT_BYTES = 48 * 1024 * 1024

GRID_W = 64
HEAD = 128
CHUNK = 64
PAIR = 2 * CHUNK
DN_CONV_W = 5
CF_CONV_W = 31
NT_DIMS = (((1,), (1,)), ((), ()))


def _cparams(sem):
    return pltpu.CompilerParams(dimension_semantics=sem, vmem_limit_bytes=VMEM_LIMIT_BYTES)


def _silu(x):
    return x * jax.nn.sigmoid(x)


def _split_bf16(x):
    hi = x.astype(BF16)
    lo = (x - hi.astype(F32)).astype(BF16)
    return hi, lo


def _ada_kernel(c_ref, w_ref, b_ref, o_ref):
    c = c_ref[...]
    s_hi, s_lo = _split_bf16(_silu(c))
    w_hi, w_lo = _split_bf16(w_ref[...])
    acc = jnp.dot(s_hi, w_hi, preferred_element_type=F32)
    acc += jnp.dot(s_lo, w_hi, preferred_element_type=F32)
    acc += jnp.dot(s_hi, w_lo, preferred_element_type=F32)
    o_ref[...] = acc + b_ref[...]


def _ada_mods(cond, ada_w, ada_b, tn=512):
    depth, d, n = ada_w.shape
    rows = cond.shape[0]
    return pl.pallas_call(
        _ada_kernel,
        grid=(depth, n // tn),
        in_specs=[
            pl.BlockSpec((rows, d), lambda i, j: (0, 0)),
            pl.BlockSpec((None, d, tn), lambda i, j: (i, 0, j)),
            pl.BlockSpec((None, 1, tn), lambda i, j: (i, 0, j)),
        ],
        out_specs=pl.BlockSpec((None, rows, tn), lambda i, j: (i, 0, j)),
        out_shape=jax.ShapeDtypeStruct((depth, rows, n), F32),
        compiler_params=_cparams(("parallel", "parallel")),
        name="ada_mods",
    )(cond, ada_w, ada_b.reshape(depth, 1, n))


def _proj_kernel(x_ref, nw_ref, mod_ref, w_ref, *rest, d, has_tail):
    if has_tail:
        wt_ref, o_ref, ot_ref, h_ref = rest
    else:
        o_ref, h_ref = rest

    @pl.when(pl.program_id(1) == 0)
    def _():
        x = x_ref[...]
        y = x * lax.rsqrt(jnp.mean(x * x, axis=-1, keepdims=True) + EPS) * nw_ref[...]
        shift = mod_ref[0, :, 0:d]
        scale = mod_ref[0, :, d:2 * d]
        h = (y * (1.0 + scale) + shift).astype(BF16)
        h_ref[...] = h
        if has_tail:
            ot_ref[...] = jnp.dot(h, wt_ref[...].astype(BF16), preferred_element_type=F32)

    o_ref[...] = jnp.dot(h_ref[...], w_ref[...].astype(BF16), preferred_element_type=F32)


def _proj(x2, seq_len, mods, norm_w, w, layer, n_main, w_tail=None, tm=1024, tn=512):
    t, d = x2.shape
    tm = min(tm, t)
    per_batch = mods.shape[0] > 1
    if per_batch:
        tm = math.gcd(tm, seq_len)
    mod_idx = (lambda i, j: ((i * tm) // seq_len, 0, 0)) if per_batch else (lambda i, j: (0, 0, 0))
    has_tail = w_tail is not None
    in_specs = [
        pl.BlockSpec((tm, d), lambda i, j: (i, 0)),
        pl.BlockSpec((1, d), lambda i, j: (0, 0)),
        pl.BlockSpec((1, 1, 3 * d), mod_idx),
        pl.BlockSpec((None, d, tn), lambda i, j: (layer, 0, j)),
    ]
    out_specs = [pl.BlockSpec((tm, tn), lambda i, j: (i, j))]
    out_shape = [jax.ShapeDtypeStruct((t, n_main), F32)]
    args = [x2, norm_w.reshape(1, d), mods, w]
    if has_tail:
        in_specs.append(pl.BlockSpec((d, LANES), lambda i, j: (0, 0)))
        out_specs.append(pl.BlockSpec((tm, LANES), lambda i, j: (i, 0)))
        out_shape.append(jax.ShapeDtypeStruct((t, LANES), F32))
        args.append(w_tail)
    outs = pl.pallas_call(
        functools.partial(_proj_kernel, d=d, has_tail=has_tail),
        grid=(t // tm, n_main // tn),
        in_specs=in_specs,
        out_specs=out_specs,
        out_shape=out_shape,
        scratch_shapes=[pltpu.VMEM((tm, d), BF16)],
        compiler_params=_cparams(("parallel", "arbitrary")),
        name="proj",
    )(*args)
    return outs if has_tail else outs[0]


def _out_kernel(a_ref, w_ref, x_ref, g_ref, o_ref):
    y = jnp.dot(a_ref[...], w_ref[...].astype(BF16), preferred_element_type=F32)
    o_ref[...] = x_ref[...] + g_ref[0] * y


def _out_proj(act, w, layer, x2, seq_len, mods, tm=1024, tn=256):
    t, d = x2.shape
    k = act.shape[1]
    tm = min(tm, t)
    per_batch = mods.shape[0] > 1
    if per_batch:
        tm = math.gcd(tm, seq_len)
    gate_blk = 2 * d // tn
    mod_idx = ((lambda i, j: ((i * tm) // seq_len, 0, gate_blk + j)) if per_batch
               else (lambda i, j: (0, 0, gate_blk + j)))
    return pl.pallas_call(
        _out_kernel,
        grid=(t // tm, d // tn),
        in_specs=[
            pl.BlockSpec((tm, k), lambda i, j: (i, 0)),
            pl.BlockSpec((None, k, tn), lambda i, j: (layer, 0, j)),
            pl.BlockSpec((tm, tn), lambda i, j: (i, j)),
            pl.BlockSpec((1, 1, tn), mod_idx),
        ],
        out_specs=pl.BlockSpec((tm, tn), lambda i, j: (i, j)),
        out_shape=jax.ShapeDtypeStruct((t, d), F32),
        compiler_params=_cparams(("parallel", "arbitrary")),
        name="out_proj",
    )(act, w, x2, mods)


def _conv_rows(pad_ref, w_ref, r0, rb, taps, dil, pad_al):
    half = (taps - 1) // 2
    base = pad_al - half * dil
    acc = None
    if dil % SUBLANES == 0 and base % SUBLANES == 0:
        for k in range(taps):
            v = pad_ref[pl.ds(pl.multiple_of(r0 + base + k * dil, SUBLANES), rb), :]
            term = v * w_ref[k:k + 1, :]
            acc = term if acc is None else acc + term
    else:
        lo = (base // SUBLANES) * SUBLANES
        span = -(-(base - lo + (taps - 1) * dil + rb) // SUBLANES) * SUBLANES
        groups = {}
        for k in range(taps):
            off = base - lo + k * dil
            groups.setdefault(off % SUBLANES, []).append((k, off // SUBLANES))
        accs = []
        for lt in range(pad_ref.shape[1] // LANES):
            cols = slice(lt * LANES, (lt + 1) * LANES)
            win = pad_ref[pl.ds(pl.multiple_of(r0 + lo, SUBLANES), span), cols]
            acc = None
            for res, members in groups.items():
                rows = max(a for _, a in members) * SUBLANES + rb
                shifted = win[res:res + rows, :]
                for k, a in members:
                    term = shifted[a * SUBLANES:a * SUBLANES + rb, :] * w_ref[k:k + 1, cols]
                    acc = term if acc is None else acc + term
            accs.append(acc)
        acc = accs[0] if len(accs) == 1 else jnp.concatenate(accs, axis=1)
    return acc


def _dnconv_kernel(x_ref, w_ref, b_ref, o_ref, pad_ref, *, nb, seq, rb, pad_al, n_norm_blocks):
    tc = pad_ref.shape[1]
    zeros = jnp.zeros((pad_al, tc), F32)
    pad_ref[0:pad_al, :] = zeros
    pad_ref[pad_al + seq:pad_al + seq + pad_al, :] = zeros
    is_qk = pl.program_id(1) < n_norm_blocks
    bias = b_ref[...]

    def seq_body(s, carry):
        pad_ref[pad_al:pad_al + seq, :] = x_ref[s]

        def chunk_body(ci, carry2):
            r0 = pl.multiple_of(ci * rb, rb)
            acc = _conv_rows(pad_ref, w_ref, r0, rb, DN_CONV_W, 1, pad_al)
            y = _silu(acc + bias)
            for g in range(tc // HEAD):
                yg = y[:, g * HEAD:(g + 1) * HEAD]
                nrm = yg * lax.rsqrt(jnp.sum(yg * yg, axis=-1, keepdims=True) + EPS)
                o_ref[g, s, pl.ds(r0, rb), :] = jnp.where(is_qk, nrm, yg)
            return carry2

        lax.fori_loop(0, seq // rb, chunk_body, 0, unroll=2)
        return carry

    lax.fori_loop(0, nb, seq_body, 0)


def _dn_conv(proj, nseq, seq, conv_w, conv_b, n_ch, n_norm, nb, tc=256, rb=64):
    width = proj.shape[1]
    x3 = proj.reshape(nseq, seq, width)
    pad_al = SUBLANES
    wpad = jnp.zeros((SUBLANES, n_ch), F32).at[:DN_CONV_W].set(conv_w)
    rb = min(rb, seq)
    return pl.pallas_call(
        functools.partial(_dnconv_kernel, nb=nb, seq=seq, rb=rb, pad_al=pad_al,
                          n_norm_blocks=n_norm // tc),
        grid=(nseq // nb, n_ch // tc),
        in_specs=[
            pl.BlockSpec((nb, seq, tc), lambda s, j: (s, 0, j)),
            pl.BlockSpec((SUBLANES, tc), lambda s, j: (0, j)),
            pl.BlockSpec((1, tc), lambda s, j: (0, j)),
        ],
        out_specs=pl.BlockSpec((tc // HEAD, nb, seq, HEAD), lambda s, j: (j, s, 0, 0)),
        out_shape=jax.ShapeDtypeStruct((n_ch // HEAD, nseq, seq, HEAD), F32),
        scratch_shapes=[pltpu.VMEM((seq + 2 * pad_al, tc), F32)],
        compiler_params=_cparams(("parallel", "parallel")),
        name="dn_conv",
    )(x3, wpad, conv_b.reshape(1, n_ch))


def _cfconv_kernel(a_ref, b_ref, w_ref, bias_ref, o_ref, pad_ref, *, nb, seq, rb, dil, pad_al):
    tc = pad_ref.shape[1]
    zeros = jnp.zeros((pad_al, tc), F32)
    pad_ref[0:pad_al, :] = zeros
    pad_ref[pad_al + seq:pad_al + seq + pad_al, :] = zeros
    bias = bias_ref[...]

    def seq_body(s, carry):
        a = a_ref[s]
        pad_ref[pad_al:pad_al + seq, :] = a * jax.nn.sigmoid(b_ref[s])

        def chunk_body(ci, carry2):
            r0 = pl.multiple_of(ci * rb, rb)
            acc = _conv_rows(pad_ref, w_ref, r0, rb, CF_CONV_W, dil, pad_al)
            o_ref[s, pl.ds(r0, rb), :] = acc + bias
            return carry2

        lax.fori_loop(0, seq // rb, chunk_body, 0)
        return carry

    lax.fori_loop(0, nb, seq_body, 0)


def _cf_conv(proj, nseq, seq, conv_w, conv_b, width, dil, nb, tc, rb=64):
    x3 = proj.reshape(nseq, seq, proj.shape[1])
    half = (CF_CONV_W - 1) // 2
    pad_al = -(-(half * dil) // SUBLANES) * SUBLANES
    wrows = -(-CF_CONV_W // SUBLANES) * SUBLANES
    wpad = jnp.zeros((wrows, width), F32).at[:CF_CONV_W].set(conv_w)
    rb = min(rb, seq)
    nblk = width // tc
    out = pl.pallas_call(
        functools.partial(_cfconv_kernel, nb=nb, seq=seq, rb=rb, dil=dil, pad_al=pad_al),
        grid=(nseq // nb, nblk),
        in_specs=[
            pl.BlockSpec((nb, seq, tc), lambda s, j: (s, 0, j)),
            pl.BlockSpec((nb, seq, tc), lambda s, j: (s, 0, nblk + j)),
            pl.BlockSpec((wrows, tc), lambda s, j: (0, j)),
            pl.BlockSpec((1, tc), lambda s, j: (0, j)),
        ],
        out_specs=pl.BlockSpec((nb, seq, tc), lambda s, j: (s, 0, j)),
        out_shape=jax.ShapeDtypeStruct((nseq, seq, width), F32),
        scratch_shapes=[pltpu.VMEM((seq + 2 * pad_al, tc), F32)],
        compiler_params=_cparams(("parallel", "parallel")),
        name="cf_conv",
    )(x3, x3, wpad, conv_b.reshape(1, width))
    return out.reshape(nseq * seq, width)


def _cfconv_sub_kernel(a_ref, b_ref, w_ref, bias_ref, o_ref, u_s, xt_s, ot_s, *, seq, groups, cb):
    tc = u_s.shape[1]
    half = (CF_CONV_W - 1) // 2
    zeros = jnp.zeros((half, SUBLANES, tc), F32)
    xt_s[0:half] = zeros
    xt_s[half + seq:half + seq + half] = zeros
    u_s[...] = a_ref[...] * jax.nn.sigmoid(b_ref[...])
    bias = bias_ref[...]

    def group_body(g, carry):
        base = pl.multiple_of(g * (SUBLANES * seq), SUBLANES * seq)

        def gather(pos, c2):
            xt_s[half + pos] = u_s[pl.ds(base + pos, SUBLANES, stride=seq), :]
            return c2

        lax.fori_loop(0, seq, gather, 0, unroll=8)

        def conv_block(ci, c2):
            p0 = ci * cb
            for lt in range(tc // LANES):
                cols = slice(lt * LANES, (lt + 1) * LANES)
                win = [xt_s[p0 + j, :, cols] for j in range(cb + CF_CONV_W - 1)]
                for j in range(cb):
                    acc = win[j] * w_ref[0:1, cols]
                    for k in range(1, CF_CONV_W):
                        acc = acc + win[j + k] * w_ref[k:k + 1, cols]
                    ot_s[p0 + j, :, cols] = acc + bias[:, cols]
            return c2

        lax.fori_loop(0, seq // cb, conv_block, 0)

        def scatter(pos, c2):
            o_ref[pl.ds(base + pos, SUBLANES, stride=seq), :] = ot_s[pos]
            return c2

        lax.fori_loop(0, seq, scatter, 0, unroll=8)
        return carry

    lax.fori_loop(0, groups, group_body, 0)


def _cf_conv_sub(proj, seq, conv_w, conv_b, width, rows, tc=LANES, cb=8):
    t = proj.shape[0]
    rows = min(rows, t)
    groups = rows // (SUBLANES * seq)
    half = (CF_CONV_W - 1) // 2
    wrows = -(-CF_CONV_W // SUBLANES) * SUBLANES
    wpad = jnp.zeros((wrows, width), F32).at[:CF_CONV_W].set(conv_w)
    nblk = width // tc
    return pl.pallas_call(
        functools.partial(_cfconv_sub_kernel, seq=seq, groups=groups, cb=cb),
        grid=(t // rows, nblk),
        in_specs=[
            pl.BlockSpec((rows, tc), lambda s, j: (s, j)),
            pl.BlockSpec((rows, tc), lambda s, j: (s, nblk + j)),
            pl.BlockSpec((wrows, tc), lambda s, j: (0, j)),
            pl.BlockSpec((1, tc), lambda s, j: (0, j)),
        ],
        out_specs=pl.BlockSpec((rows, tc), lambda s, j: (s, j)),
        out_shape=jax.ShapeDtypeStruct((t, width), F32),
        scratch_shapes=[pltpu.VMEM((rows, tc), F32),
                        pltpu.VMEM((seq + 2 * half, SUBLANES, tc), F32),
                        pltpu.VMEM((seq, SUBLANES, tc), F32)],
        compiler_params=_cparams(("parallel", "parallel")),
        name="cf_conv_sub",
    )(proj, proj, wpad, conv_b.reshape(1, width))


def _cfgate_kernel(u_ref, z_ref, w_ref, b_ref, o_ref):
    u = u_ref[...]
    mu = jnp.mean(u, axis=-1, keepdims=True)
    uc = u - mu
    var = jnp.mean(uc * uc, axis=-1, keepdims=True)
    y = uc * lax.rsqrt(var + EPS) * w_ref[...] + b_ref[...]
    o_ref[...] = (_silu(y) * _silu(z_ref[...])).astype(BF16)


def _cf_gate(uc, proj, ln_w, ln_b, tm=256):
    t, width = uc.shape
    tm = min(tm, t)
    return pl.pallas_call(
        _cfgate_kernel,
        grid=(t // tm,),
        in_specs=[
            pl.BlockSpec((tm, width), lambda i: (i, 0)),
            pl.BlockSpec((tm, width), lambda i: (i, 2)),
            pl.BlockSpec((1, width), lambda i: (0, 0)),
            pl.BlockSpec((1, width), lambda i: (0, 0)),
        ],
        out_specs=pl.BlockSpec((tm, width), lambda i: (i, 0)),
        out_shape=jax.ShapeDtypeStruct((t, width), BF16),
        compiler_params=_cparams(("parallel",)),
        name="cf_gate",
    )(uc, proj, ln_w.reshape(1, width), ln_b.reshape(1, width))


def _dngate_kernel(of_ref, ob_ref, z_ref, w_ref, o_ref, *, hb):
    for h in range(hb):
        o = of_ref[h] + ob_ref[h]
        o = o * lax.rsqrt(jnp.mean(o * o, axis=-1, keepdims=True) + EPS) * w_ref[...]
        z = z_ref[:, h * HEAD:(h + 1) * HEAD]
        o_ref[:, h * HEAD:(h + 1) * HEAD] = (o * _silu(z)).astype(BF16)


def _dn_gate(o_f, o_b, proj, z_col0, norm_w, tm=512, hb=4):
    nh, t, _ = o_f.shape
    tm = min(tm, t)
    zblk = z_col0 // (hb * HEAD)
    return pl.pallas_call(
        functools.partial(_dngate_kernel, hb=hb),
        grid=(t // tm, nh // hb),
        in_specs=[
            pl.BlockSpec((hb, tm, HEAD), lambda i, j: (j, i, 0)),
            pl.BlockSpec((hb, tm, HEAD), lambda i, j: (j, i, 0)),
            pl.BlockSpec((tm, hb * HEAD), lambda i, j: (i, zblk + j)),
            pl.BlockSpec((1, HEAD), lambda i, j: (0, 0)),
        ],
        out_specs=pl.BlockSpec((tm, hb * HEAD), lambda i, j: (i, j)),
        out_shape=jax.ShapeDtypeStruct((t, nh * HEAD), BF16),
        compiler_params=_cparams(("parallel", "parallel")),
        name="dn_gate",
    )(o_f, o_b, proj, norm_w.reshape(1, HEAD))


def _block_diag(a, b):
    za = jnp.zeros((a.shape[0], b.shape[1]), a.dtype)
    zb = jnp.zeros((b.shape[0], a.shape[1]), a.dtype)
    return jnp.concatenate([jnp.concatenate([a, za], axis=1), jnp.concatenate([zb, b], axis=1)], axis=0)


def _log_decay_and_beta(ab, alog_row, dtb_row):
    x = ab + dtb_row
    softplus = jnp.maximum(x, 0.0) + jnp.log1p(jnp.exp(-jnp.abs(x)))
    return -jnp.exp(alog_row) * softplus, jax.nn.sigmoid(ab)


def _chunk_cumsum(g_all):
    row = lax.broadcasted_iota(jnp.int32, (CHUNK, CHUNK), 0)
    col = lax.broadcasted_iota(jnp.int32, (CHUNK, CHUNK), 1)
    lane = lax.broadcasted_iota(jnp.int32, (CHUNK, LANES), 1)
    gam_f = jnp.dot((row >= col).astype(F32), g_all, precision=HI, preferred_element_type=F32)
    gam_b = jnp.dot((row <= col).astype(F32), g_all, precision=HI, preferred_element_type=F32)
    return jnp.where(lane < LANES // 2, gam_f, gam_b)


def _gdr_inv_kernel(k_ref, ab_ref, alog_ref, dtb_ref, z_ref, gs_s, gl_s, tl_s, a_s, beta_s, gam_s, *, nqk, ncs):
    lane = lax.broadcasted_iota(jnp.int32, (CHUNK, LANES), 1)
    sub8 = lax.broadcasted_iota(jnp.int32, (SUBLANES, LANES), 0)
    half = LANES // 2

    beta_l = [jnp.zeros((CHUNK, LANES), F32)] * 2
    gam_l = [jnp.zeros((CHUNK, LANES), F32)] * 2
    for c in range(ncs):
        g_all, beta_all = _log_decay_and_beta(ab_ref[c * CHUNK:(c + 1) * CHUNK, :], alog_ref[...], dtb_ref[...])
        gam = _chunk_cumsum(g_all)
        for r in range(2):
            dst = r * half + c * nqk
            sel = lax.shift_right_logical(lane, 4) == dst // nqk
            for d in range(2):
                src_b = d * half + r * nqk
                src_g = src_b + 2 * nqk
                beta_l[d] = jnp.where(sel, pltpu.roll(beta_all, (dst - src_b) % LANES, 1), beta_l[d])
                gam_l[d] = jnp.where(sel, pltpu.roll(gam, (dst - src_g) % LANES, 1), gam_l[d])
    for d in range(2):
        beta_s[d] = beta_l[d]
        gam_s[d] = gam_l[d]

    def gram_body(c, carry):
        r0 = pl.multiple_of(c * CHUNK, CHUNK)
        for hp in range(nqk // 2):
            ka = k_ref[2 * hp, pl.ds(r0, CHUNK), :].astype(BF16)
            kb = k_ref[2 * hp + 1, pl.ds(r0, CHUNK), :].astype(BF16)
            gg = lax.dot_general(jnp.concatenate([ka, kb], axis=0), jnp.concatenate([ka, ka, kb, kb], axis=0),
                                 NT_DIMS, preferred_element_type=F32)
            for e in range(2):
                g_dup = gg[e * CHUNK:(e + 1) * CHUNK, e * PAIR:(e + 1) * PAIR]
                g_pk = jnp.where(lane < CHUNK, g_dup, pltpu.roll(g_dup, CHUNK - 1, 0))
                row0 = pl.multiple_of((c * nqk + 2 * hp + e) * CHUNK, CHUNK)
                gs_s[pl.ds(row0, CHUNK), :] = g_pk
        return carry

    lax.fori_loop(0, ncs, gram_body, 0)

    def gt_body(ip, carry):
        m = gs_s[pl.ds(2 * ip, half, stride=CHUNK), :]
        mt = jnp.concatenate([m, m], axis=0).T
        gl_s[2 * ip] = mt[:CHUNK]
        gl_s[2 * ip + 1] = mt[CHUNK:]
        return carry

    lax.fori_loop(0, CHUNK // 2, gt_body, 0)

    sq_row = lax.broadcasted_iota(jnp.int32, (LANES, LANES), 0)
    sq_col = lax.broadcasted_iota(jnp.int32, (LANES, LANES), 1)
    keep = lax.shift_right_logical(sq_col, 6) == lax.shift_right_logical(sq_row, 6)
    eye_b = (sq_row == sq_col).astype(BF16)
    rows_per_dot = 4
    nblk = CHUNK // SUBLANES

    for d in range(2):
        fwd = d == 0
        for i in (range(CHUNK) if fwd else range(CHUNK - 1, -1, -1)):
            before = range(0, i) if fwd else range(i + 1, CHUNK)
            ib = i // SUBLANES
            own_blocks = range(0, ib + 1) if fwd else range(ib, nblk)
            acc = {cb: ((sub8 + cb * SUBLANES) == i).astype(F32) if cb == ib else jnp.zeros((SUBLANES, LANES), F32)
                   for cb in own_blocks}
            if len(before):
                g_i = gam_s[d, i:i + 1, :]
                b_i = beta_s[d, i:i + 1, :]
                for jb in sorted({j // SUBLANES for j in before}):
                    js = sub8 + jb * SUBLANES
                    valid = (js < i) if fwd else (js > i)
                    g_j = gam_s[d, jb * SUBLANES:(jb + 1) * SUBLANES, :]
                    dec = jnp.exp(jnp.where(valid, g_i - g_j, -jnp.inf))
                    a_s[jb * SUBLANES:(jb + 1) * SUBLANES, :] = (
                        gl_s[i, jb * SUBLANES:(jb + 1) * SUBLANES, :] * (b_i * dec))
                for j in before:
                    a_ij = a_s[j:j + 1, :]
                    jbk = j // SUBLANES
                    for cb in (range(0, jbk + 1) if fwd else range(jbk, nblk)):
                        acc[cb] = acc[cb] - a_ij * tl_s[j, cb * SUBLANES:(cb + 1) * SUBLANES, :]
            for cb in range(nblk):
                tl_s[i, cb * SUBLANES:(cb + 1) * SUBLANES, :] = acc.get(cb, jnp.zeros((SUBLANES, LANES), F32))

        def tb_body(ig, carry):
            slabs = []
            for t in range(rows_per_dot):
                slab = tl_s[ig * rows_per_dot + t].astype(BF16)
                slabs += [slab, slab]
            st_all = lax.dot_general(eye_b, jnp.concatenate(slabs, axis=0), NT_DIMS,
                                     preferred_element_type=F32)
            for t in range(rows_per_dot):
                st = jnp.where(keep, st_all[:, t * LANES:(t + 1) * LANES], 0.0)
                for c in range(ncs):
                    for r in range(2):
                        z_ref[d, 0, c, ig * rows_per_dot + t, r] = st[r * half + c * nqk:r * half + (c + 1) * nqk]
            return carry

        lax.fori_loop(0, CHUNK // rows_per_dot, tb_body, 0, unroll=2)


def _gdr_inv(qkv, ab, alog_row, dtb_row):
    ng, t, _ = qkv.shape
    nqk = ng // 4
    ncs = LANES // (2 * nqk)
    nc = t // CHUNK
    rows = ncs * CHUNK
    z = pl.pallas_call(
        functools.partial(_gdr_inv_kernel, nqk=nqk, ncs=ncs),
        grid=(nc // ncs,),
        in_specs=[
            pl.BlockSpec((nqk, rows, HEAD), lambda s: (1, s, 0)),
            pl.BlockSpec((rows, LANES), lambda s: (s, 0)),
            pl.BlockSpec((1, LANES), lambda s: (0, 0)),
            pl.BlockSpec((1, LANES), lambda s: (0, 0)),
        ],
        out_specs=pl.BlockSpec((2, 1, ncs, CHUNK, 2, nqk, LANES), lambda s: (0, s, 0, 0, 0, 0, 0)),
        out_shape=jax.ShapeDtypeStruct((2, nc // ncs, ncs, CHUNK, 2, nqk, LANES), F32),
        scratch_shapes=[
            pltpu.VMEM((ncs * nqk * CHUNK, LANES), F32),
            pltpu.VMEM((CHUNK, CHUNK, LANES), F32),
            pltpu.VMEM((CHUNK, CHUNK, LANES), F32),
            pltpu.VMEM((CHUNK, LANES), F32),
            pltpu.VMEM((2, CHUNK, LANES), F32),
            pltpu.VMEM((2, CHUNK, LANES), F32),
        ],
        compiler_params=_cparams(("parallel",)),
        name="gdr_inv",
    )(qkv, ab, alog_row, dtb_row)
    return z.reshape(2, nc, CHUNK * 2 * nqk, LANES)


def _gdr_prep_kernel(q_ref, k_ref, v_ref, ab_ref, t_ref, alog_ref, dtb_ref, wq_ref, u_ref, ak_ref,
                     gam_s, beta_s, gamt_s, *, hqb, nqk):
    nv = 2 * nqk
    prow = lax.broadcasted_iota(jnp.int32, (PAIR, PAIR), 0)
    pcol = lax.broadcasted_iota(jnp.int32, (PAIR, PAIR), 1)
    same_prob = lax.shift_right_logical(prow, 6) == lax.shift_right_logical(pcol, 6)
    incl = (same_prob & (prow >= pcol), same_prob & (prow <= pcol))
    eye_l = (prow == pcol).astype(F32)
    lane1 = lax.broadcasted_iota(jnp.int32, (1, PAIR), 1)
    scale = HEAD ** -0.5

    g_all, beta_all = _log_decay_and_beta(ab_ref[...], alog_ref[...], dtb_ref[...])
    gam = _chunk_cumsum(g_all)
    gam_s[...] = gam
    beta_s[...] = beta_all
    gam2 = jnp.concatenate([gam, pltpu.roll(gam, LANES - nqk, 1)], axis=0)
    gamt_s[...] = lax.dot_general(eye_l, gam2, NT_DIMS, precision=HI, preferred_element_type=F32)

    def block_body(blk, carry):
        shift = (LANES - blk * hqb) % LANES
        gam_r = pltpu.roll(gam_s[...], shift, 1)
        beta_r = pltpu.roll(beta_s[...], shift, 1)
        gam_t = [gamt_s[pl.ds(pl.multiple_of(d * 2 * nv + nv + blk * hqb, hqb), hqb), :] for d in range(2)]
        for i in range(hqb):
            hq = blk * hqb + i
            k = k_ref[hq]
            qs = q_ref[hq] * scale
            k2 = jnp.concatenate([k, k], axis=0)
            gq = lax.dot_general(jnp.concatenate([qs.astype(BF16), eye_l.astype(BF16)], axis=0), k2.astype(BF16),
                                 NT_DIMS, preferred_element_type=F32)
            qk2, kt2 = gq[:CHUNK], gq[CHUNK:]
            q2 = jnp.concatenate([qs, qs], axis=0)
            v2 = jnp.concatenate([v_ref[2 * hq], v_ref[2 * hq + 1]], axis=0)
            rhs, qg, glrow = [], [], []
            for d in range(2):
                lane_b = d * 2 * nv + i
                lane_g = lane_b + nv
                cg = jnp.concatenate([gam_r[:, lane_g:lane_g + 1], gam_r[:, lane_g + nqk:lane_g + nqk + 1]], axis=0)
                cb = jnp.concatenate([beta_r[:, lane_b:lane_b + 1], beta_r[:, lane_b + nqk:lane_b + nqk + 1]], axis=0)
                rg = gam_t[d][i:i + 1, :]
                last = CHUNK - 1 if d == 0 else 0
                gl = jnp.where(lane1 < CHUNK, rg[:, last:last + 1], rg[:, CHUNK + last:CHUNK + last + 1])
                cg = jnp.broadcast_to(cg, (PAIR, HEAD))
                cb = jnp.broadcast_to(cb, (PAIR, HEAD))
                dec = jnp.exp(jnp.where(incl[d], cg - rg, -jnp.inf))
                eg = jnp.exp(cg)
                rhs.append(jnp.concatenate([v2 * cb, k2 * (cb * eg)], axis=1).astype(BF16))
                qg.append(q2 * eg)
                glrow.append(gl)
                dsum = dec[:CHUNK] + dec[CHUNK:]
                ak_ref[d, 0, hq] = jnp.concatenate([qk2 * dsum, kt2 * jnp.exp(gl - rg)], axis=0).astype(BF16)
            t_pair = [jnp.concatenate([t_ref[d, 0, pl.ds(hq, CHUNK, stride=nv), :],
                                       t_ref[d, 0, pl.ds(nqk + hq, CHUNK, stride=nv), :]], axis=0).astype(BF16)
                      for d in range(2)]
            tq = _block_diag(t_pair[0], t_pair[1])
            uw = jnp.dot(tq, jnp.concatenate(rhs, axis=0), preferred_element_type=F32)
            for d in range(2):
                uw_d = uw[d * PAIR:(d + 1) * PAIR]
                u2, w2 = uw_d[:, :HEAD], uw_d[:, HEAD:]
                wq_a = jnp.concatenate([w2[:CHUNK], qg[d][:CHUNK]], axis=0)
                wq_b = jnp.concatenate([w2[CHUNK:], qg[d][CHUNK:]], axis=0)
                wq_ref[d, 0, hq] = jnp.concatenate([wq_a, wq_b], axis=1).astype(BF16)
                u_ref[d, 0, hq, 0:PAIR, :] = u2
                egl = jnp.exp(glrow[d])
                u_ref[d, 0, hq, PAIR:PAIR + SUBLANES, :] = jnp.broadcast_to(egl[:, 0:1], (SUBLANES, HEAD))
                u_ref[d, 0, hq, PAIR + SUBLANES:PAIR + 2 * SUBLANES, :] = jnp.broadcast_to(
                    egl[:, CHUNK:CHUNK + 1], (SUBLANES, HEAD))
        return carry

    lax.fori_loop(0, nqk // hqb, block_body, 0)


def _gdr_prep(qkv, ab, t_inv, alog_row, dtb_row, hqb=8):
    ng, t, _ = qkv.shape
    nqk = ng // 4
    nv = 2 * nqk
    nc = t // CHUNK
    urows = PAIR + 2 * SUBLANES
    return pl.pallas_call(
        functools.partial(_gdr_prep_kernel, hqb=hqb, nqk=nqk),
        grid=(nc,),
        in_specs=[
            pl.BlockSpec((nqk, CHUNK, HEAD), lambda c: (0, c, 0)),
            pl.BlockSpec((nqk, CHUNK, HEAD), lambda c: (1, c, 0)),
            pl.BlockSpec((nv, CHUNK, HEAD), lambda c: (1, c, 0)),
            pl.BlockSpec((CHUNK, LANES), lambda c: (c, 0)),
            pl.BlockSpec((2, 1, CHUNK * nv, LANES), lambda c: (0, c, 0, 0)),
            pl.BlockSpec((1, LANES), lambda c: (0, 0)),
            pl.BlockSpec((1, LANES), lambda c: (0, 0)),
        ],
        out_specs=[
            pl.BlockSpec((2, 1, nqk, PAIR, 2 * HEAD), lambda c: (0, c, 0, 0, 0)),
            pl.BlockSpec((2, 1, nqk, urows, HEAD), lambda c: (0, c, 0, 0, 0)),
            pl.BlockSpec((2, 1, nqk, CHUNK + HEAD, PAIR), lambda c: (0, c, 0, 0, 0)),
        ],
        out_shape=[
            jax.ShapeDtypeStruct((2, nc, nqk, PAIR, 2 * HEAD), BF16),
            jax.ShapeDtypeStruct((2, nc, nqk, urows, HEAD), F32),
            jax.ShapeDtypeStruct((2, nc, nqk, CHUNK + HEAD, PAIR), BF16),
        ],
        scratch_shapes=[pltpu.VMEM((CHUNK, LANES), F32), pltpu.VMEM((CHUNK, LANES), F32),
                        pltpu.VMEM((LANES, PAIR), F32)],
        compiler_params=_cparams(("parallel",)),
        name="gdr_prep",
    )(qkv, qkv, qkv, ab, t_inv, alog_row, dtb_row)


def _gdr_scan_kernel(*refs, nqk, n_chunks, has_s0):
    wqf_ref, uf_ref, akf_ref, wqb_ref, ub_ref, akb_ref = refs[:6]
    if has_s0:
        s0_ref = refs[6]
        of_ref, ob_ref, sout_ref, s_scr = refs[7:]
    else:
        of_ref, ob_ref, sout_ref, s_scr = refs[6:]
    n = pl.program_id(1)

    @pl.when(n == 0)
    def _():
        if has_s0:
            s_scr[...] = s0_ref[0]
        else:
            s_scr[...] = jnp.zeros(s_scr.shape, F32)

    dirs = ((wqf_ref, uf_ref, of_ref), (wqb_ref, ub_ref, ob_ref))
    zero_v = jnp.zeros((CHUNK, HEAD), BF16)
    for hq in range(nqk):
        m1s, v_rows = [], []
        for d, (wq_ref, u_ref, _) in enumerate(dirs):
            s_a = s_scr[d, 2 * hq]
            s_b = s_scr[d, 2 * hq + 1]
            m1 = jnp.dot(wq_ref[0, 0, hq], _block_diag(s_a.astype(BF16), s_b.astype(BF16)),
                         preferred_element_type=F32)
            m1s.append(m1)
            for r in range(2):
                p = 2 * d + r
                v_new = (u_ref[0, 0, hq, r * CHUNK:(r + 1) * CHUNK, :]
                         - m1[:CHUNK, r * HEAD:(r + 1) * HEAD]).astype(BF16)
                v_rows.append(jnp.concatenate([v_new if c == p else zero_v for c in range(4)], axis=1))
        ak = jnp.concatenate([akf_ref[0, 0, hq], akb_ref[0, 0, hq]], axis=1)
        m2 = jnp.dot(ak, jnp.concatenate(v_rows, axis=0), preferred_element_type=F32)
        for d, (_, u_ref, o_ref) in enumerate(dirs):
            for r in range(2):
                p = 2 * d + r
                hv = 2 * hq + r
                m2_p = m2[:, p * HEAD:(p + 1) * HEAD]
                o_ref[hv] = m1s[d][CHUNK:, r * HEAD:(r + 1) * HEAD] + m2_p[:CHUNK]
                decay = u_ref[0, 0, hq, PAIR + r * SUBLANES:PAIR + r * SUBLANES + 1, :]
                s_scr[d, hv] = s_scr[d, hv] * decay + m2_p[CHUNK:]

    @pl.when(n == n_chunks - 1)
    def _():
        sout_ref[0] = s_scr[...]


def _gdr_scan(wq, u, ak, s0, nseq, seq):
    _, nc, nqk, _, _ = wq.shape
    nv = 2 * nqk
    n = seq // CHUNK
    t = nc * CHUNK
    urows = u.shape[3]
    has_s0 = s0 is not None
    fwd = lambda s, c: s * n + c
    bwd = lambda s, c: s * n + (n - 1 - c)

    def specs(d, cb):
        return [
            pl.BlockSpec((1, 1, nqk, PAIR, 2 * HEAD), lambda s, c: (d, cb(s, c), 0, 0, 0)),
            pl.BlockSpec((1, 1, nqk, urows, HEAD), lambda s, c: (d, cb(s, c), 0, 0, 0)),
            pl.BlockSpec((1, 1, nqk, CHUNK + HEAD, PAIR), lambda s, c: (d, cb(s, c), 0, 0, 0)),
        ]

    in_specs = specs(0, fwd) + specs(1, bwd)
    args = [wq, u, ak, wq, u, ak]
    if has_s0:
        in_specs.append(pl.BlockSpec((1, 2, nv, HEAD, HEAD), lambda s, c: (s, 0, 0, 0, 0)))
        args.append(s0)
    o_f, o_b, s_new = pl.pallas_call(
        functools.partial(_gdr_scan_kernel, nqk=nqk, n_chunks=n, has_s0=has_s0),
        grid=(nseq, n),
        in_specs=in_specs,
        out_specs=[
            pl.BlockSpec((nv, CHUNK, HEAD), lambda s, c: (0, fwd(s, c), 0)),
            pl.BlockSpec((nv, CHUNK, HEAD), lambda s, c: (0, bwd(s, c), 0)),
            pl.BlockSpec((1, 2, nv, HEAD, HEAD), lambda s, c: (s, 0, 0, 0, 0)),
        ],
        out_shape=[
            jax.ShapeDtypeStruct((nv, t, HEAD), F32),
            jax.ShapeDtypeStruct((nv, t, HEAD), F32),
            jax.ShapeDtypeStruct((nseq, 2, nv, HEAD, HEAD), F32),
        ],
        scratch_shapes=[pltpu.VMEM((2, nv, HEAD, HEAD), F32)],
        compiler_params=_cparams(("parallel", "arbitrary")),
        name="gdr_scan",
    )(*args)
    return o_f, o_b, s_new


def _gdr(qkv, ab, alog_row, dtb_row, s0, nseq, seq):
    t_inv = _gdr_inv(qkv, ab, alog_row, dtb_row)
    wq, u, ak = _gdr_prep(qkv, ab, t_inv, alog_row, dtb_row)
    return _gdr_scan(wq, u, ak, s0, nseq, seq)


def _rms_kernel(x_ref, w_ref, o_ref):
    x = x_ref[...]
    o_ref[...] = x * lax.rsqrt(jnp.mean(x * x, axis=-1, keepdims=True) + EPS) * w_ref[...]


def _final_norm(x2, w, tm=512):
    t, d = x2.shape
    tm = min(tm, t)
    return pl.pallas_call(
        _rms_kernel,
        grid=(t // tm,),
        in_specs=[pl.BlockSpec((tm, d), lambda i: (i, 0)), pl.BlockSpec((1, d), lambda i: (0, 0))],
        out_specs=pl.BlockSpec((tm, d), lambda i: (i, 0)),
        out_shape=jax.ShapeDtypeStruct((t, d), F32),
        compiler_params=_cparams(("parallel",)),
        name="final_norm",
    )(x2, w.reshape(1, d))


def _permute_ab_cols(w_tail, nqk):
    lead = w_tail.shape[:-1]
    w = w_tail.reshape(lead + (2, 2, nqk, 2))
    return jnp.swapaxes(w, -1, -2).reshape(lead + (8 * nqk,))


def _alpha_row(p, nqk):
    full = jnp.stack([jnp.zeros_like(p), p], axis=1)
    return _permute_ab_cols(full.reshape(1, -1), nqk)


def _dn_layer(x, mods, norm_w, layer, w_in, conv_w, conv_b, a_log, dt_bias, head_norm_w, w_out, s0, nb_conv):
    b, seq, d = x.shape
    x2 = x.reshape(b * seq, d)
    val_dim = w_out.shape[1]
    nv = val_dim // HEAD
    nqk = nv // 2
    key_dim = nqk * HEAD
    conv_ch = 2 * key_dim + val_dim
    n_main = conv_ch + val_dim
    w_tail = _permute_ab_cols(w_in[layer, :, n_main:], nqk)
    proj, ab = _proj(x2, seq, mods, norm_w, w_in, layer, n_main, w_tail)
    qkv = _dn_conv(proj, b, seq, conv_w, conv_b, conv_ch, 2 * key_dim, nb_conv)
    qkv = qkv.reshape(conv_ch // HEAD, b * seq, HEAD)
    o_f, o_b, s_new = _gdr(qkv, ab, _alpha_row(a_log, nqk), _alpha_row(dt_bias, nqk), s0, b, seq)
    act = _dn_gate(o_f, o_b, proj, conv_ch, head_norm_w)
    xn = _out_proj(act, w_out, layer, x2, seq, mods)
    return xn.reshape(b, seq, d), s_new


def _cf_layer(x, mods, norm_w, layer, w_in, conv_w, conv_b, ln_w, ln_b, w_out, mode):
    b, seq, d = x.shape
    x2 = x.reshape(b * seq, d)
    width = w_out.shape[1]
    proj = _proj(x2, seq, mods, norm_w, w_in, layer, 3 * width)
    if mode == "seq":
        uc = _cf_conv_sub(proj, seq, conv_w, conv_b, width, rows=SUBLANES * seq)
    elif mode == "row":
        uc = _cf_conv_sub(proj, GRID_W, conv_w, conv_b, width, rows=4 * SUBLANES * GRID_W)
    else:
        uc = _cf_conv(proj, b, seq, conv_w, conv_b, width, GRID_W, nb=1, tc=128)
    act = _cf_gate(uc, proj, ln_w, ln_b)
    xn = _out_proj(act, w_out, layer, x2, seq, mods)
    return xn.reshape(b, seq, d)


def _to_col_major(x):
    b, seq, ch = x.shape
    rows = seq // GRID_W
    return x.reshape(b, rows, GRID_W, ch).transpose(0, 2, 1, 3).reshape(b, seq, ch)


def _to_row_major(x):
    b, seq, ch = x.shape
    rows = seq // GRID_W
    return x.reshape(b, GRID_W, rows, ch).transpose(0, 2, 1, 3).reshape(b, seq, ch)


def kernel(x_prompt, x_sample, state_l0, state_l2, c, c_ctx, ada_w, ada_b, norm_w, dn_w_in, dn_conv_w,
           dn_conv_b, dn_a_log, dn_dt_bias, dn_norm_w, dn_w_out, cf_w_in, cf_conv_w, cf_conv_b, cf_ln_w,
           cf_ln_b, cf_w_out, final_norm_w):
    depth = ada_w.shape[0]
    bs = x_sample.shape[0]
    d = x_prompt.shape[-1]
    rows = -(-(1 + bs) // SUBLANES) * SUBLANES
    cond = jnp.zeros((rows, d), F32).at[0].set(c_ctx).at[1:1 + bs].set(c)
    mods = _ada_mods(cond, ada_w, ada_b)
    caches = (state_l0, state_l2)
    new_states = []
    xp, xs = x_prompt, x_sample
    for i in range(depth):
        j = i // 2
        mp = mods[i, 0:1][:, None, :]
        ms = mods[i, 1:1 + bs][:, None, :]
        if i % 2 == 0:
            dn = (j, dn_w_in, dn_conv_w[j], dn_conv_b[j], dn_a_log[j], dn_dt_bias[j], dn_norm_w[j], dn_w_out)
            xp, sp = _dn_layer(xp, mp, norm_w[i], *dn, s0=None, nb_conv=min(8, xp.shape[0]))
            new_states.append(sp)
            if j % 2 == 1:
                xs_cm, _ = _dn_layer(_to_col_major(xs), ms, norm_w[i], *dn, s0=caches[j], nb_conv=1)
                xs = _to_row_major(xs_cm)
            else:
                xs, _ = _dn_layer(xs, ms, norm_w[i], *dn, s0=caches[j], nb_conv=1)
        else:
            cf = (j, cf_w_in, cf_conv_w[j], cf_conv_b[j], cf_ln_w[j], cf_ln_b[j], cf_w_out)
            xp = _cf_layer(xp, mp, norm_w[i], *cf, mode="seq")
            xs = _cf_layer(xs, ms, norm_w[i], *cf, mode="row" if j % 2 == 0 else "col")
    bp, sp_len, _ = xp.shape
    y_prompt = _final_norm(xp.reshape(bp * sp_len, d), final_norm_w).reshape(xp.shape)
    y_sample = _final_norm(xs.reshape(bs * xs.shape[1], d), final_norm_w).reshape(xs.shape)
    return (y_prompt, y_sample, new_states[0], new_states[1])
```

```python
import functools
import math

import jax
import jax.numpy as jnp
from jax import lax
from jax.experimental import pallas as pl
from jax.experimental.pallas import tpu as pltpu

F32 = jnp.float32
BF16 = jnp.bfloat16
EPS = 1e-6
HI = lax.Precision.HIGHEST

LANES = 128
SUBLANES = 8
VMEM_LIMIT_BYTES = 48 * 1024 * 1024

GRID_W = 64
HEAD = 128
CHUNK = 64
PAIR = 2 * CHUNK
DN_CONV_W = 5
CF_CONV_W = 31
NT_DIMS = (((1,), (1,)), ((), ()))


def _cparams(sem):
    return pltpu.CompilerParams(dimension_semantics=sem, vmem_limit_bytes=VMEM_LIMIT_BYTES)


def _silu(x):
    return x * jax.nn.sigmoid(x)


def _split_bf16(x):
    hi = x.astype(BF16)
    lo = (x - hi.astype(F32)).astype(BF16)
    return hi, lo


def _ada_kernel(c_ref, w_ref, b_ref, o_ref):
    c = c_ref[...]
    s_hi, s_lo = _split_bf16(_silu(c))
    w_hi, w_lo = _split_bf16(w_ref[...])
    acc = jnp.dot(s_hi, w_hi, preferred_element_type=F32)
    acc += jnp.dot(s_lo, w_hi, preferred_element_type=F32)
    acc += jnp.dot(s_hi, w_lo, preferred_element_type=F32)
    o_ref[...] = acc + b_ref[...]


def _ada_mods(cond, ada_w, ada_b, tn=512):
    depth, d, n = ada_w.shape
    rows = cond.shape[0]
    return pl.pallas_call(
        _ada_kernel,
        grid=(depth, n // tn),
        in_specs=[
            pl.BlockSpec((rows, d), lambda i, j: (0, 0)),
            pl.BlockSpec((None, d, tn), lambda i, j: (i, 0, j)),
            pl.BlockSpec((None, 1, tn), lambda i, j: (i, 0, j)),
        ],
        out_specs=pl.BlockSpec((None, rows, tn), lambda i, j: (i, 0, j)),
        out_shape=jax.ShapeDtypeStruct((depth, rows, n), F32),
        compiler_params=_cparams(("parallel", "parallel")),
        name="ada_mods",
    )(cond, ada_w, ada_b.reshape(depth, 1, n))


def _proj_kernel(x_ref, nw_ref, mod_ref, w_ref, *rest, d, has_tail):
    if has_tail:
        wt_ref, o_ref, ot_ref, h_ref = rest
    else:
        o_ref, h_ref = rest

    @pl.when(pl.program_id(1) == 0)
    def _():
        x = x_ref[...]
        y = x * lax.rsqrt(jnp.mean(x * x, axis=-1, keepdims=True) + EPS) * nw_ref[...]
        shift = mod_ref[0, :, 0:d]
        scale = mod_ref[0, :, d:2 * d]
        h = (y * (1.0 + scale) + shift).astype(BF16)
        h_ref[...] = h
        if has_tail:
            ot_ref[...] = jnp.dot(h, wt_ref[...].astype(BF16), preferred_element_type=F32)

    o_ref[...] = jnp.dot(h_ref[...], w_ref[...].astype(BF16), preferred_element_type=F32)


def _proj(x2, seq_len, mods, norm_w, w, layer, n_main, w_tail=None, tm=1024, tn=512):
    t, d = x2.shape
    tm = min(tm, t)
    per_batch = mods.shape[0] > 1
    if per_batch:
        tm = math.gcd(tm, seq_len)
    mod_idx = (lambda i, j: ((i * tm) // seq_len, 0, 0)) if per_batch else (lambda i, j: (0, 0, 0))
    has_tail = w_tail is not None
    in_specs = [
        pl.BlockSpec((tm, d), lambda i, j: (i, 0)),
        pl.BlockSpec((1, d), lambda i, j: (0, 0)),
        pl.BlockSpec((1, 1, 3 * d), mod_idx),
        pl.BlockSpec((None, d, tn), lambda i, j: (layer, 0, j)),
    ]
    out_specs = [pl.BlockSpec((tm, tn), lambda i, j: (i, j))]
    out_shape = [jax.ShapeDtypeStruct((t, n_main), F32)]
    args = [x2, norm_w.reshape(1, d), mods, w]
    if has_tail:
        in_specs.append(pl.BlockSpec((d, LANES), lambda i, j: (0, 0)))
        out_specs.append(pl.BlockSpec((tm, LANES), lambda i, j: (i, 0)))
        out_shape.append(jax.ShapeDtypeStruct((t, LANES), F32))
        args.append(w_tail)
    outs = pl.pallas_call(
        functools.partial(_proj_kernel, d=d, has_tail=has_tail),
        grid=(t // tm, n_main // tn),
        in_specs=in_specs,
        out_specs=out_specs,
        out_shape=out_shape,
        scratch_shapes=[pltpu.VMEM((tm, d), BF16)],
        compiler_params=_cparams(("parallel", "arbitrary")),
        name="proj",
    )(*args)
    return outs if has_tail else outs[0]


def _out_kernel(a_ref, w_ref, x_ref, g_ref, o_ref):
    y = jnp.dot(a_ref[...], w_ref[...].astype(BF16), preferred_element_type=F32)
    o_ref[...] = x_ref[...] + g_ref[0] * y


def _out_proj(act, w, layer, x2, seq_len, mods, tm=1024, tn=256):
    t, d = x2.shape
    k = act.shape[1]
    tm = min(tm, t)
    per_batch = mods.shape[0] > 1
    if per_batch:
        tm = math.gcd(tm, seq_len)
    gate_blk = 2 * d // tn
    mod_idx = ((lambda i, j: ((i * tm) // seq_len, 0, gate_blk + j)) if per_batch
               else (lambda i, j: (0, 0, gate_blk + j)))
    return pl.pallas_call(
        _out_kernel,
        grid=(t // tm, d // tn),
        in_specs=[
            pl.BlockSpec((tm, k), lambda i, j: (i, 0)),
            pl.BlockSpec((None, k, tn), lambda i, j: (layer, 0, j)),
            pl.BlockSpec((tm, tn), lambda i, j: (i, j)),
            pl.BlockSpec((1, 1, tn), mod_idx),
        ],
        out_specs=pl.BlockSpec((tm, tn), lambda i, j: (i, j)),
        out_shape=jax.ShapeDtypeStruct((t, d), F32),
        compiler_params=_cparams(("parallel", "arbitrary")),
        name="out_proj",
    )(act, w, x2, mods)


def _conv_rows(pad_ref, w_ref, r0, rb, taps, dil, pad_al):
    half = (taps - 1) // 2
    base = pad_al - half * dil
    acc = None
    if dil % SUBLANES == 0 and base % SUBLANES == 0:
        for k in range(taps):
            v = pad_ref[pl.ds(pl.multiple_of(r0 + base + k * dil, SUBLANES), rb), :]
            term = v * w_ref[k:k + 1, :]
            acc = term if acc is None else acc + term
    else:
        lo = (base // SUBLANES) * SUBLANES
        span = -(-(base - lo + (taps - 1) * dil + rb) // SUBLANES) * SUBLANES
        groups = {}
        for k in range(taps):
            off = base - lo + k * dil
            groups.setdefault(off % SUBLANES, []).append((k, off // SUBLANES))
        accs = []
        for lt in range(pad_ref.shape[1] // LANES):
            cols = slice(lt * LANES, (lt + 1) * LANES)
            win = pad_ref[pl.ds(pl.multiple_of(r0 + lo, SUBLANES), span), cols]
            acc = None
            for res, members in groups.items():
                rows = max(a for _, a in members) * SUBLANES + rb
                shifted = win[res:res + rows, :]
                for k, a in members:
                    term = shifted[a * SUBLANES:a * SUBLANES + rb, :] * w_ref[k:k + 1, cols]
                    acc = term if acc is None else acc + term
            accs.append(acc)
        acc = accs[0] if len(accs) == 1 else jnp.concatenate(accs, axis=1)
    return acc


def _dnconv_kernel(x_ref, w_ref, b_ref, o_ref, pad_ref, *, nb, seq, rb, pad_al, n_norm_blocks):
    tc = pad_ref.shape[1]
    zeros = jnp.zeros((pad_al, tc), F32)
    pad_ref[0:pad_al, :] = zeros
    pad_ref[pad_al + seq:pad_al + seq + pad_al, :] = zeros
    is_qk = pl.program_id(1) < n_norm_blocks
    bias = b_ref[...]

    def seq_body(s, carry):
        pad_ref[pad_al:pad_al + seq, :] = x_ref[s]

        def chunk_body(ci, carry2):
            r0 = pl.multiple_of(ci * rb, rb)
            acc = _conv_rows(pad_ref, w_ref, r0, rb, DN_CONV_W, 1, pad_al)
            y = _silu(acc + bias)
            for g in range(tc // HEAD):
                yg = y[:, g * HEAD:(g + 1) * HEAD]
                nrm = yg * lax.rsqrt(jnp.sum(yg * yg, axis=-1, keepdims=True) + EPS)
                o_ref[g, s, pl.ds(r0, rb), :] = jnp.where(is_qk, nrm, yg)
            return carry2

        lax.fori_loop(0, seq // rb, chunk_body, 0, unroll=2)
        return carry

    lax.fori_loop(0, nb, seq_body, 0)


def _dn_conv(proj, nseq, seq, conv_w, conv_b, n_ch, n_norm, nb, tc=256, rb=64):
    width = proj.shape[1]
    x3 = proj.reshape(nseq, seq, width)
    pad_al = SUBLANES
    wpad = jnp.zeros((SUBLANES, n_ch), F32).at[:DN_CONV_W].set(conv_w)
    rb = min(rb, seq)
    return pl.pallas_call(
        functools.partial(_dnconv_kernel, nb=nb, seq=seq, rb=rb, pad_al=pad_al,
                          n_norm_blocks=n_norm // tc),
        grid=(nseq // nb, n_ch // tc),
        in_specs=[
            pl.BlockSpec((nb, seq, tc), lambda s, j: (s, 0, j)),
            pl.BlockSpec((SUBLANES, tc), lambda s, j: (0, j)),
            pl.BlockSpec((1, tc), lambda s, j: (0, j)),
        ],
        out_specs=pl.BlockSpec((tc // HEAD, nb, seq, HEAD), lambda s, j: (j, s, 0, 0)),
        out_shape=jax.ShapeDtypeStruct((n_ch // HEAD, nseq, seq, HEAD), F32),
        scratch_shapes=[pltpu.VMEM((seq + 2 * pad_al, tc), F32)],
        compiler_params=_cparams(("parallel", "parallel")),
        name="dn_conv",
    )(x3, wpad, conv_b.reshape(1, n_ch))


def _cfconv_kernel(a_ref, b_ref, w_ref, bias_ref, o_ref, pad_ref, *, nb, seq, rb, dil, pad_al):
    tc = pad_ref.shape[1]
    zeros = jnp.zeros((pad_al, tc), F32)
    pad_ref[0:pad_al, :] = zeros
    pad_ref[pad_al + seq:pad_al + seq + pad_al, :] = zeros
    bias = bias_ref[...]

    def seq_body(s, carry):
        a = a_ref[s]
        pad_ref[pad_al:pad_al + seq, :] = a * jax.nn.sigmoid(b_ref[s])

        def chunk_body(ci, carry2):
            r0 = pl.multiple_of(ci * rb, rb)
            acc = _conv_rows(pad_ref, w_ref, r0, rb, CF_CONV_W, dil, pad_al)
            o_ref[s, pl.ds(r0, rb), :] = acc + bias
            return carry2

        lax.fori_loop(0, seq // rb, chunk_body, 0)
        return carry

    lax.fori_loop(0, nb, seq_body, 0)


def _cf_conv(proj, nseq, seq, conv_w, conv_b, width, dil, nb, tc, rb=64):
    x3 = proj.reshape(nseq, seq, proj.shape[1])
    half = (CF_CONV_W - 1) // 2
    pad_al = -(-(half * dil) // SUBLANES) * SUBLANES
    wrows = -(-CF_CONV_W // SUBLANES) * SUBLANES
    wpad = jnp.zeros((wrows, width), F32).at[:CF_CONV_W].set(conv_w)
    rb = min(rb, seq)
    nblk = width // tc
    out = pl.pallas_call(
        functools.partial(_cfconv_kernel, nb=nb, seq=seq, rb=rb, dil=dil, pad_al=pad_al),
        grid=(nseq // nb, nblk),
        in_specs=[
            pl.BlockSpec((nb, seq, tc), lambda s, j: (s, 0, j)),
            pl.BlockSpec((nb, seq, tc), lambda s, j: (s, 0, nblk + j)),
            pl.BlockSpec((wrows, tc), lambda s, j: (0, j)),
            pl.BlockSpec((1, tc), lambda s, j: (0, j)),
        ],
        out_specs=pl.BlockSpec((nb, seq, tc), lambda s, j: (s, 0, j)),
        out_shape=jax.ShapeDtypeStruct((nseq, seq, width), F32),
        scratch_shapes=[pltpu.VMEM((seq + 2 * pad_al, tc), F32)],
        compiler_params=_cparams(("parallel", "parallel")),
        name="cf_conv",
    )(x3, x3, wpad, conv_b.reshape(1, width))
    return out.reshape(nseq * seq, width)


def _cfconv_sub_kernel(a_ref, b_ref, w_ref, bias_ref, o_ref, u_s, xt_s, ot_s, *, seq, groups, cb):
    tc = u_s.shape[1]
    half = (CF_CONV_W - 1) // 2
    zeros = jnp.zeros((half, SUBLANES, tc), F32)
    xt_s[0:half] = zeros
    xt_s[half + seq:half + seq + half] = zeros
    u_s[...] = a_ref[...] * jax.nn.sigmoid(b_ref[...])
    bias = bias_ref[...]

    def group_body(g, carry):
        base = pl.multiple_of(g * (SUBLANES * seq), SUBLANES * seq)

        def gather(pos, c2):
            xt_s[half + pos] = u_s[pl.ds(base + pos, SUBLANES, stride=seq), :]
            return c2

        lax.fori_loop(0, seq, gather, 0, unroll=8)

        def conv_block(ci, c2):
            p0 = ci * cb
            for lt in range(tc // LANES):
                cols = slice(lt * LANES, (lt + 1) * LANES)
                win = [xt_s[p0 + j, :, cols] for j in range(cb + CF_CONV_W - 1)]
                for j in range(cb):
                    acc = win[j] * w_ref[0:1, cols]
                    for k in range(1, CF_CONV_W):
                        acc = acc + win[j + k] * w_ref[k:k + 1, cols]
                    ot_s[p0 + j, :, cols] = acc + bias[:, cols]
            return c2

        lax.fori_loop(0, seq // cb, conv_block, 0, unroll=2)

        def scatter(pos, c2):
            o_ref[pl.ds(base + pos, SUBLANES, stride=seq), :] = ot_s[pos]
            return c2

        lax.fori_loop(0, seq, scatter, 0, unroll=8)
        return carry

    lax.fori_loop(0, groups, group_body, 0)


def _cf_conv_sub(proj, seq, conv_w, conv_b, width, rows, tc=LANES, cb=8):
    t = proj.shape[0]
    rows = min(rows, t)
    groups = rows // (SUBLANES * seq)
    half = (CF_CONV_W - 1) // 2
    wrows = -(-CF_CONV_W // SUBLANES) * SUBLANES
    wpad = jnp.zeros((wrows, width), F32).at[:CF_CONV_W].set(conv_w)
    nblk = width // tc
    return pl.pallas_call(
        functools.partial(_cfconv_sub_kernel, seq=seq, groups=groups, cb=cb),
        grid=(t // rows, nblk),
        in_specs=[
            pl.BlockSpec((rows, tc), lambda s, j: (s, j)),
            pl.BlockSpec((rows, tc), lambda s, j: (s, nblk + j)),
            pl.BlockSpec((wrows, tc), lambda s, j: (0, j)),
            pl.BlockSpec((1, tc), lambda s, j: (0, j)),
        ],
        out_specs=pl.BlockSpec((rows, tc), lambda s, j: (s, j)),
        out_shape=jax.ShapeDtypeStruct((t, width), F32),
        scratch_shapes=[pltpu.VMEM((rows, tc), F32),
                        pltpu.VMEM((seq + 2 * half, SUBLANES, tc), F32),
                        pltpu.VMEM((seq, SUBLANES, tc), F32)],
        compiler_params=_cparams(("parallel", "parallel")),
        name="cf_conv_sub",
    )(proj, proj, wpad, conv_b.reshape(1, width))


def _cfgate_kernel(u_ref, z_ref, w_ref, b_ref, o_ref):
    u = u_ref[...]
    mu = jnp.mean(u, axis=-1, keepdims=True)
    uc = u - mu
    var = jnp.mean(uc * uc, axis=-1, keepdims=True)
    y = uc * lax.rsqrt(var + EPS) * w_ref[...] + b_ref[...]
    o_ref[...] = (_silu(y) * _silu(z_ref[...])).astype(BF16)


def _cf_gate(uc, proj, ln_w, ln_b, tm=256):
    t, width = uc.shape
    tm = min(tm, t)
    return pl.pallas_call(
        _cfgate_kernel,
        grid=(t // tm,),
        in_specs=[
            pl.BlockSpec((tm, width), lambda i: (i, 0)),
            pl.BlockSpec((tm, width), lambda i: (i, 2)),
            pl.BlockSpec((1, width), lambda i: (0, 0)),
            pl.BlockSpec((1, width), lambda i: (0, 0)),
        ],
        out_specs=pl.BlockSpec((tm, width), lambda i: (i, 0)),
        out_shape=jax.ShapeDtypeStruct((t, width), BF16),
        compiler_params=_cparams(("parallel",)),
        name="cf_gate",
    )(uc, proj, ln_w.reshape(1, width), ln_b.reshape(1, width))


def _dngate_kernel(of_ref, ob_ref, z_ref, w_ref, o_ref, *, hb):
    for h in range(hb):
        o = of_ref[h] + ob_ref[h]
        o = o * lax.rsqrt(jnp.mean(o * o, axis=-1, keepdims=True) + EPS) * w_ref[...]
        z = z_ref[:, h * HEAD:(h + 1) * HEAD]
        o_ref[:, h * HEAD:(h + 1) * HEAD] = (o * _silu(z)).astype(BF16)


def _dn_gate(o_f, o_b, proj, z_col0, norm_w, tm=512, hb=4):
    nh, t, _ = o_f.shape
    tm = min(tm, t)
    zblk = z_col0 // (hb * HEAD)
    return pl.pallas_call(
        functools.partial(_dngate_kernel, hb=hb),
        grid=(t // tm, nh // hb),
        in_specs=[
            pl.BlockSpec((hb, tm, HEAD), lambda i, j: (j, i, 0)),
            pl.BlockSpec((hb, tm, HEAD), lambda i, j: (j, i, 0)),
            pl.BlockSpec((tm, hb * HEAD), lambda i, j: (i, zblk + j)),
            pl.BlockSpec((1, HEAD), lambda i, j: (0, 0)),
        ],
        out_specs=pl.BlockSpec((tm, hb * HEAD), lambda i, j: (i, j)),
        out_shape=jax.ShapeDtypeStruct((t, nh * HEAD), BF16),
        compiler_params=_cparams(("parallel", "parallel")),
        name="dn_gate",
    )(o_f, o_b, proj, norm_w.reshape(1, HEAD))


def _block_diag(a, b):
    za = jnp.zeros((a.shape[0], b.shape[1]), a.dtype)
    zb = jnp.zeros((b.shape[0], a.shape[1]), a.dtype)
    return jnp.concatenate([jnp.concatenate([a, za], axis=1), jnp.concatenate([zb, b], axis=1)], axis=0)


def _log_decay_and_beta(ab, alog_row, dtb_row):
    x = ab + dtb_row
    softplus = jnp.maximum(x, 0.0) + jnp.log1p(jnp.exp(-jnp.abs(x)))
    return -jnp.exp(alog_row) * softplus, jax.nn.sigmoid(ab)


def _chunk_cumsum(g_all):
    row = lax.broadcasted_iota(jnp.int32, (CHUNK, CHUNK), 0)
    col = lax.broadcasted_iota(jnp.int32, (CHUNK, CHUNK), 1)
    lane = lax.broadcasted_iota(jnp.int32, (CHUNK, LANES), 1)
    gam_f = jnp.dot((row >= col).astype(F32), g_all, precision=HI, preferred_element_type=F32)
    gam_b = jnp.dot((row <= col).astype(F32), g_all, precision=HI, preferred_element_type=F32)
    return jnp.where(lane < LANES // 2, gam_f, gam_b)


def _gdr_inv_kernel(k_ref, ab_ref, alog_ref, dtb_ref, z_ref, gs_s, gl_s, tl_s, a_s, beta_s, gam_s, *, nqk, ncs):
    lane = lax.broadcasted_iota(jnp.int32, (CHUNK, LANES), 1)
    sub8 = lax.broadcasted_iota(jnp.int32, (SUBLANES, LANES), 0)
    half = LANES // 2

    beta_l = [jnp.zeros((CHUNK, LANES), F32)] * 2
    gam_l = [jnp.zeros((CHUNK, LANES), F32)] * 2
    for c in range(ncs):
        g_all, beta_all = _log_decay_and_beta(ab_ref[c * CHUNK:(c + 1) * CHUNK, :], alog_ref[...], dtb_ref[...])
        gam = _chunk_cumsum(g_all)
        for r in range(2):
            dst = r * half + c * nqk
            sel = lax.shift_right_logical(lane, 4) == dst // nqk
            for d in range(2):
                src_b = d * half + r * nqk
                src_g = src_b + 2 * nqk
                beta_l[d] = jnp.where(sel, pltpu.roll(beta_all, (dst - src_b) % LANES, 1), beta_l[d])
                gam_l[d] = jnp.where(sel, pltpu.roll(gam, (dst - src_g) % LANES, 1), gam_l[d])
    for d in range(2):
        beta_s[d] = beta_l[d]
        gam_s[d] = gam_l[d]

    def gram_body(c, carry):
        r0 = pl.multiple_of(c * CHUNK, CHUNK)
        for hp in range(nqk // 2):
            ka = k_ref[2 * hp, pl.ds(r0, CHUNK), :].astype(BF16)
            kb = k_ref[2 * hp + 1, pl.ds(r0, CHUNK), :].astype(BF16)
            gg = lax.dot_general(jnp.concatenate([ka, kb], axis=0), jnp.concatenate([ka, ka, kb, kb], axis=0),
                                 NT_DIMS, preferred_element_type=F32)
            for e in range(2):
                g_dup = gg[e * CHUNK:(e + 1) * CHUNK, e * PAIR:(e + 1) * PAIR]
                g_pk = jnp.where(lane < CHUNK, g_dup, pltpu.roll(g_dup, CHUNK - 1, 0))
                row0 = pl.multiple_of((c * nqk + 2 * hp + e) * CHUNK, CHUNK)
                gs_s[pl.ds(row0, CHUNK), :] = g_pk
        return carry

    lax.fori_loop(0, ncs, gram_body, 0)

    sq_row = lax.broadcasted_iota(jnp.int32, (LANES, LANES), 0)
    sq_col = lax.broadcasted_iota(jnp.int32, (LANES, LANES), 1)
    keep = lax.shift_right_logical(sq_col, 6) == lax.shift_right_logical(sq_row, 6)
    eye_b = (sq_row == sq_col).astype(BF16)

    def gt_body(iq, carry):
        pieces = []
        for t in range(2):
            m = gs_s[pl.ds(2 * (2 * iq + t), half, stride=CHUNK), :]
            hi, lo = _split_bf16(m)
            pieces += [hi, hi, lo, lo]
        out = lax.dot_general(eye_b, jnp.concatenate(pieces, axis=0), NT_DIMS, preferred_element_type=F32)
        for t in range(2):
            mt = out[:, 2 * t * LANES:(2 * t + 1) * LANES] + out[:, (2 * t + 1) * LANES:(2 * t + 2) * LANES]
            gl_s[2 * (2 * iq + t)] = mt[:CHUNK]
            gl_s[2 * (2 * iq + t) + 1] = mt[CHUNK:]
        return carry

    lax.fori_loop(0, CHUNK // 4, gt_body, 0, unroll=2)
    rows_per_dot = 4
    nblk = CHUNK // SUBLANES

    for d in range(2):
        fwd = d == 0
        for i in (range(CHUNK) if fwd else range(CHUNK - 1, -1, -1)):
            before = range(0, i) if fwd else range(i + 1, CHUNK)
            ib = i // SUBLANES
            own_blocks = range(0, ib + 1) if fwd else range(ib, nblk)
            acc = {cb: ((sub8 + cb * SUBLANES) == i).astype(F32) if cb == ib else jnp.zeros((SUBLANES, LANES), F32)
                   for cb in own_blocks}
            if len(before):
                g_i = gam_s[d, i:i + 1, :]
                b_i = beta_s[d, i:i + 1, :]
                for jb in sorted({j // SUBLANES for j in before}):
                    js = sub8 + jb * SUBLANES
                    valid = (js < i) if fwd else (js > i)
                    g_j = gam_s[d, jb * SUBLANES:(jb + 1) * SUBLANES, :]
                    dec = jnp.exp(jnp.where(valid, g_i - g_j, -jnp.inf))
                    a_s[jb * SUBLANES:(jb + 1) * SUBLANES, :] = (
                        gl_s[i, jb * SUBLANES:(jb + 1) * SUBLANES, :] * (b_i * dec))
                for j in before:
                    a_ij = a_s[j:j + 1, :]
                    jbk = j // SUBLANES
                    for cb in (range(0, jbk + 1) if fwd else range(jbk, nblk)):
                        acc[cb] = acc[cb] - a_ij * tl_s[j, cb * SUBLANES:(cb + 1) * SUBLANES, :]
            for cb in range(nblk):
                tl_s[i, cb * SUBLANES:(cb + 1) * SUBLANES, :] = acc.get(cb, jnp.zeros((SUBLANES, LANES), F32))

        def tb_body(ig, carry):
            slabs = []
            for t in range(rows_per_dot):
                slab = tl_s[ig * rows_per_dot + t].astype(BF16)
                slabs += [slab, slab]
            st_all = lax.dot_general(eye_b, jnp.concatenate(slabs, axis=0), NT_DIMS,
                                     preferred_element_type=F32)
            for t in range(rows_per_dot):
                st = jnp.where(keep, st_all[:, t * LANES:(t + 1) * LANES], 0.0)
                for c in range(ncs):
                    for r in range(2):
                        z_ref[d, 0, c, ig * rows_per_dot + t, r] = st[r * half + c * nqk:r * half + (c + 1) * nqk]
            return carry

        lax.fori_loop(0, CHUNK // rows_per_dot, tb_body, 0, unroll=2)


def _gdr_inv(qkv, ab, alog_row, dtb_row):
    ng, t, _ = qkv.shape
    nqk = ng // 4
    ncs = LANES // (2 * nqk)
    nc = t // CHUNK
    rows = ncs * CHUNK
    z = pl.pallas_call(
        functools.partial(_gdr_inv_kernel, nqk=nqk, ncs=ncs),
        grid=(nc // ncs,),
        in_specs=[
            pl.BlockSpec((nqk, rows, HEAD), lambda s: (1, s, 0)),
            pl.BlockSpec((rows, LANES), lambda s: (s, 0)),
            pl.BlockSpec((1, LANES), lambda s: (0, 0)),
            pl.BlockSpec((1, LANES), lambda s: (0, 0)),
        ],
        out_specs=pl.BlockSpec((2, 1, ncs, CHUNK, 2, nqk, LANES), lambda s: (0, s, 0, 0, 0, 0, 0)),
        out_shape=jax.ShapeDtypeStruct((2, nc // ncs, ncs, CHUNK, 2, nqk, LANES), F32),
        scratch_shapes=[
            pltpu.VMEM((ncs * nqk * CHUNK, LANES), F32),
            pltpu.VMEM((CHUNK, CHUNK, LANES), F32),
            pltpu.VMEM((CHUNK, CHUNK, LANES), F32),
            pltpu.VMEM((CHUNK, LANES), F32),
            pltpu.VMEM((2, CHUNK, LANES), F32),
            pltpu.VMEM((2, CHUNK, LANES), F32),
        ],
        compiler_params=_cparams(("parallel",)),
        name="gdr_inv",
    )(qkv, ab, alog_row, dtb_row)
    return z.reshape(2, nc, CHUNK * 2 * nqk, LANES)


def _gdr_prep_kernel(q_ref, k_ref, v_ref, ab_ref, t_ref, alog_ref, dtb_ref, wq_ref, u_ref, ak_ref,
                     gam_s, beta_s, gamt_s, *, hqb, nqk):
    nv = 2 * nqk
    prow = lax.broadcasted_iota(jnp.int32, (PAIR, PAIR), 0)
    pcol = lax.broadcasted_iota(jnp.int32, (PAIR, PAIR), 1)
    same_prob = lax.shift_right_logical(prow, 6) == lax.shift_right_logical(pcol, 6)
    incl = (same_prob & (prow >= pcol), same_prob & (prow <= pcol))
    eye_l = (prow == pcol).astype(F32)
    lane1 = lax.broadcasted_iota(jnp.int32, (1, PAIR), 1)
    scale = HEAD ** -0.5

    g_all, beta_all = _log_decay_and_beta(ab_ref[...], alog_ref[...], dtb_ref[...])
    gam = _chunk_cumsum(g_all)
    gam_s[...] = gam
    beta_s[...] = beta_all
    gam2 = jnp.concatenate([gam, pltpu.roll(gam, LANES - nqk, 1)], axis=0)
    gamt_s[...] = lax.dot_general(eye_l, gam2, NT_DIMS, precision=HI, preferred_element_type=F32)

    def block_body(blk, carry):
        shift = (LANES - blk * hqb) % LANES
        gam_r = pltpu.roll(gam_s[...], shift, 1)
        beta_r = pltpu.roll(beta_s[...], shift, 1)
        gam_t = [gamt_s[pl.ds(pl.multiple_of(d * 2 * nv + nv + blk * hqb, hqb), hqb), :] for d in range(2)]
        for i in range(hqb):
            hq = blk * hqb + i
            k = k_ref[hq]
            qs = q_ref[hq] * scale
            k2 = jnp.concatenate([k, k], axis=0)
            gq = lax.dot_general(jnp.concatenate([qs.astype(BF16), eye_l.astype(BF16)], axis=0), k2.astype(BF16),
                                 NT_DIMS, preferred_element_type=F32)
            qk2, kt2 = gq[:CHUNK], gq[CHUNK:]
            q2 = jnp.concatenate([qs, qs], axis=0)
            v2 = jnp.concatenate([v_ref[2 * hq], v_ref[2 * hq + 1]], axis=0)
            rhs, qg, glrow = [], [], []
            for d in range(2):
                lane_b = d * 2 * nv + i
                lane_g = lane_b + nv
                cg = jnp.concatenate([gam_r[:, lane_g:lane_g + 1], gam_r[:, lane_g + nqk:lane_g + nqk + 1]], axis=0)
                cb = jnp.concatenate([beta_r[:, lane_b:lane_b + 1], beta_r[:, lane_b + nqk:lane_b + nqk + 1]], axis=0)
                rg = gam_t[d][i:i + 1, :]
                last = CHUNK - 1 if d == 0 else 0
                gl = jnp.where(lane1 < CHUNK, rg[:, last:last + 1], rg[:, CHUNK + last:CHUNK + last + 1])
                cg = jnp.broadcast_to(cg, (PAIR, HEAD))
                cb = jnp.broadcast_to(cb, (PAIR, HEAD))
                dec = jnp.exp(jnp.where(incl[d], cg - rg, -jnp.inf))
                eg = jnp.exp(cg)
                rhs.append(jnp.concatenate([v2 * cb, k2 * (cb * eg)], axis=1).astype(BF16))
                qg.append(q2 * eg)
                glrow.append(gl)
                dsum = dec[:CHUNK] + dec[CHUNK:]
                ak_ref[d, 0, hq] = jnp.concatenate([qk2 * dsum, kt2 * jnp.exp(gl - rg)], axis=0).astype(BF16)
            t_pair = [jnp.concatenate([t_ref[d, 0, pl.ds(hq, CHUNK, stride=nv), :],
                                       t_ref[d, 0, pl.ds(nqk + hq, CHUNK, stride=nv), :]], axis=0).astype(BF16)
                      for d in range(2)]
            tq = _block_diag(t_pair[0], t_pair[1])
            uw = jnp.dot(tq, jnp.concatenate(rhs, axis=0), preferred_element_type=F32)
            for d in range(2):
                uw_d = uw[d * PAIR:(d + 1) * PAIR]
                u2, w2 = uw_d[:, :HEAD], uw_d[:, HEAD:]
                wq_a = jnp.concatenate([w2[:CHUNK], qg[d][:CHUNK]], axis=0)
                wq_b = jnp.concatenate([w2[CHUNK:], qg[d][CHUNK:]], axis=0)
                wq_ref[d, 0, hq] = jnp.concatenate([wq_a, wq_b], axis=1).astype(BF16)
                u_ref[d, 0, hq, 0:PAIR, :] = u2
                egl = jnp.exp(glrow[d])
                u_ref[d, 0, hq, PAIR:PAIR + SUBLANES, :] = jnp.broadcast_to(egl[:, 0:1], (SUBLANES, HEAD))
                u_ref[d, 0, hq, PAIR + SUBLANES:PAIR + 2 * SUBLANES, :] = jnp.broadcast_to(
                    egl[:, CHUNK:CHUNK + 1], (SUBLANES, HEAD))
        return carry

    lax.fori_loop(0, nqk // hqb, block_body, 0)


def _gdr_prep(qkv, ab, t_inv, alog_row, dtb_row, hqb=8):
    ng, t, _ = qkv.shape
    nqk = ng // 4
    nv = 2 * nqk
    nc = t // CHUNK
    urows = PAIR + 2 * SUBLANES
    return pl.pallas_call(
        functools.partial(_gdr_prep_kernel, hqb=hqb, nqk=nqk),
        grid=(nc,),
        in_specs=[
            pl.BlockSpec((nqk, CHUNK, HEAD), lambda c: (0, c, 0)),
            pl.BlockSpec((nqk, CHUNK, HEAD), lambda c: (1, c, 0)),
            pl.BlockSpec((nv, CHUNK, HEAD), lambda c: (1, c, 0)),
            pl.BlockSpec((CHUNK, LANES), lambda c: (c, 0)),
            pl.BlockSpec((2, 1, CHUNK * nv, LANES), lambda c: (0, c, 0, 0)),
            pl.BlockSpec((1, LANES), lambda c: (0, 0)),
            pl.BlockSpec((1, LANES), lambda c: (0, 0)),
        ],
        out_specs=[
            pl.BlockSpec((2, 1, nqk, PAIR, 2 * HEAD), lambda c: (0, c, 0, 0, 0)),
            pl.BlockSpec((2, 1, nqk, urows, HEAD), lambda c: (0, c, 0, 0, 0)),
            pl.BlockSpec((2, 1, nqk, CHUNK + HEAD, PAIR), lambda c: (0, c, 0, 0, 0)),
        ],
        out_shape=[
            jax.ShapeDtypeStruct((2, nc, nqk, PAIR, 2 * HEAD), BF16),
            jax.ShapeDtypeStruct((2, nc, nqk, urows, HEAD), F32),
            jax.ShapeDtypeStruct((2, nc, nqk, CHUNK + HEAD, PAIR), BF16),
        ],
        scratch_shapes=[pltpu.VMEM((CHUNK, LANES), F32), pltpu.VMEM((CHUNK, LANES), F32),
                        pltpu.VMEM((LANES, PAIR), F32)],
        compiler_params=_cparams(("parallel",)),
        name="gdr_prep",
    )(qkv, qkv, qkv, ab, t_inv, alog_row, dtb_row)


def _gdr_scan_kernel(*refs, nqk, n_chunks, has_s0):
    wqf_ref, uf_ref, akf_ref, wqb_ref, ub_ref, akb_ref = refs[:6]
    if has_s0:
        s0_ref = refs[6]
        of_ref, ob_ref, sout_ref, s_scr = refs[7:]
    else:
        of_ref, ob_ref, sout_ref, s_scr = refs[6:]
    n = pl.program_id(1)

    @pl.when(n == 0)
    def _():
        if has_s0:
            s_scr[...] = s0_ref[0]
        else:
            s_scr[...] = jnp.zeros(s_scr.shape, F32)

    dirs = ((wqf_ref, uf_ref, of_ref), (wqb_ref, ub_ref, ob_ref))
    zero_v = jnp.zeros((CHUNK, HEAD), BF16)
    zero_s = jnp.zeros((HEAD, HEAD), BF16)
    for hq in range(nqk):
        s_rows = []
        for d in range(2):
            for r in range(2):
                s_p = s_scr[d, 2 * hq + r].astype(BF16)
                s_rows.append(jnp.concatenate([s_p if c == 2 * d + r else zero_s for c in range(4)], axis=1))
        wq = jnp.concatenate([wqf_ref[0, 0, hq], wqb_ref[0, 0, hq]], axis=1)
        m1 = jnp.dot(wq, jnp.concatenate(s_rows, axis=0), preferred_element_type=F32)
        v_rows = []
        for d, (_, u_ref, _) in enumerate(dirs):
            for r in range(2):
                p = 2 * d + r
                v_new = (u_ref[0, 0, hq, r * CHUNK:(r + 1) * CHUNK, :]
                         - m1[:CHUNK, p * HEAD:(p + 1) * HEAD]).astype(BF16)
                v_rows.append(jnp.concatenate([v_new if c == p else zero_v for c in range(4)], axis=1))
        ak = jnp.concatenate([akf_ref[0, 0, hq], akb_ref[0, 0, hq]], axis=1)
        m2 = jnp.dot(ak, jnp.concatenate(v_rows, axis=0), preferred_element_type=F32)
        for d, (_, u_ref, o_ref) in enumerate(dirs):
            for r in range(2):
                p = 2 * d + r
                hv = 2 * hq + r
                m2_p = m2[:, p * HEAD:(p + 1) * HEAD]
                o_ref[hv] = m1[CHUNK:, p * HEAD:(p + 1) * HEAD] + m2_p[:CHUNK]
                decay = u_ref[0, 0, hq, PAIR + r * SUBLANES:PAIR + r * SUBLANES + 1, :]
                s_scr[d, hv] = s_scr[d, hv] * decay + m2_p[CHUNK:]

    @pl.when(n == n_chunks - 1)
    def _():
        sout_ref[0] = s_scr[...]


def _gdr_scan(wq, u, ak, s0, nseq, seq):
    _, nc, nqk, _, _ = wq.shape
    nv = 2 * nqk
    n = seq // CHUNK
    t = nc * CHUNK
    urows = u.shape[3]
    has_s0 = s0 is not None
    fwd = lambda s, c: s * n + c
    bwd = lambda s, c: s * n + (n - 1 - c)

    def specs(d, cb):
        return [
            pl.BlockSpec((1, 1, nqk, PAIR, 2 * HEAD), lambda s, c: (d, cb(s, c), 0, 0, 0)),
            pl.BlockSpec((1, 1, nqk, urows, HEAD), lambda s, c: (d, cb(s, c), 0, 0, 0)),
            pl.BlockSpec((1, 1, nqk, CHUNK + HEAD, PAIR), lambda s, c: (d, cb(s, c), 0, 0, 0)),
        ]

    in_specs = specs(0, fwd) + specs(1, bwd)
    args = [wq, u, ak, wq, u, ak]
    if has_s0:
        in_specs.append(pl.BlockSpec((1, 2, nv, HEAD, HEAD), lambda s, c: (s, 0, 0, 0, 0)))
        args.append(s0)
    o_f, o_b, s_new = pl.pallas_call(
        functools.partial(_gdr_scan_kernel, nqk=nqk, n_chunks=n, has_s0=has_s0),
        grid=(nseq, n),
        in_specs=in_specs,
        out_specs=[
            pl.BlockSpec((nv, CHUNK, HEAD), lambda s, c: (0, fwd(s, c), 0)),
            pl.BlockSpec((nv, CHUNK, HEAD), lambda s, c: (0, bwd(s, c), 0)),
            pl.BlockSpec((1, 2, nv, HEAD, HEAD), lambda s, c: (s, 0, 0, 0, 0)),
        ],
        out_shape=[
            jax.ShapeDtypeStruct((nv, t, HEAD), F32),
            jax.ShapeDtypeStruct((nv, t, HEAD), F32),
            jax.ShapeDtypeStruct((nseq, 2, nv, HEAD, HEAD), F32),
        ],
        scratch_shapes=[pltpu.VMEM((2, nv, HEAD, HEAD), F32)],
        compiler_params=_cparams(("parallel", "arbitrary")),
        name="gdr_scan",
    )(*args)
    return o_f, o_b, s_new


def _gdr(qkv, ab, alog_row, dtb_row, s0, nseq, seq):
    t_inv = _gdr_inv(qkv, ab, alog_row, dtb_row)
    wq, u, ak = _gdr_prep(qkv, ab, t_inv, alog_row, dtb_row)
    return _gdr_scan(wq, u, ak, s0, nseq, seq)


def _rms_kernel(x_ref, w_ref, o_ref):
    x = x_ref[...]
    o_ref[...] = x * lax.rsqrt(jnp.mean(x * x, axis=-1, keepdims=True) + EPS) * w_ref[...]


def _final_norm(x2, w, tm=512):
    t, d = x2.shape
    tm = min(tm, t)
    return pl.pallas_call(
        _rms_kernel,
        grid=(t // tm,),
        in_specs=[pl.BlockSpec((tm, d), lambda i: (i, 0)), pl.BlockSpec((1, d), lambda i: (0, 0))],
        out_specs=pl.BlockSpec((tm, d), lambda i: (i, 0)),
        out_shape=jax.ShapeDtypeStruct((t, d), F32),
        compiler_params=_cparams(("parallel",)),
        name="final_norm",
    )(x2, w.reshape(1, d))


def _permute_ab_cols(w_tail, nqk):
    lead = w_tail.shape[:-1]
    w = w_tail.reshape(lead + (2, 2, nqk, 2))
    return jnp.swapaxes(w, -1, -2).reshape(lead + (8 * nqk,))


def _alpha_row(p, nqk):
    full = jnp.stack([jnp.zeros_like(p), p], axis=1)
    return _permute_ab_cols(full.reshape(1, -1), nqk)


def _dn_layer(x, mods, norm_w, layer, w_in, conv_w, conv_b, a_log, dt_bias, head_norm_w, w_out, s0, nb_conv):
    b, seq, d = x.shape
    x2 = x.reshape(b * seq, d)
    val_dim = w_out.shape[1]
    nv = val_dim // HEAD
    nqk = nv // 2
    key_dim = nqk * HEAD
    conv_ch = 2 * key_dim + val_dim
    n_main = conv_ch + val_dim
    w_tail = _permute_ab_cols(w_in[layer, :, n_main:], nqk)
    proj, ab = _proj(x2, seq, mods, norm_w, w_in, layer, n_main, w_tail)
    qkv = _dn_conv(proj, b, seq, conv_w, conv_b, conv_ch, 2 * key_dim, nb_conv)
    qkv = qkv.reshape(conv_ch // HEAD, b * seq, HEAD)
    o_f, o_b, s_new = _gdr(qkv, ab, _alpha_row(a_log, nqk), _alpha_row(dt_bias, nqk), s0, b, seq)
    act = _dn_gate(o_f, o_b, proj, conv_ch, head_norm_w)
    xn = _out_proj(act, w_out, layer, x2, seq, mods)
    return xn.reshape(b, seq, d), s_new


def _cf_layer(x, mods, norm_w, layer, w_in, conv_w, conv_b, ln_w, ln_b, w_out, mode):
    b, seq, d = x.shape
    x2 = x.reshape(b * seq, d)
    width = w_out.shape[1]
    proj = _proj(x2, seq, mods, norm_w, w_in, layer, 3 * width)
    if mode == "seq":
        uc = _cf_conv_sub(proj, seq, conv_w, conv_b, width, rows=SUBLANES * seq)
    elif mode == "row":
        uc = _cf_conv_sub(proj, GRID_W, conv_w, conv_b, width, rows=4 * SUBLANES * GRID_W)
    else:
        uc = _cf_conv(proj, b, seq, conv_w, conv_b, width, GRID_W, nb=1, tc=128)
    act = _cf_gate(uc, proj, ln_w, ln_b)
    xn = _out_proj(act, w_out, layer, x2, seq, mods)
    return xn.reshape(b, seq, d)


def _to_col_major(x):
    b, seq, ch = x.shape
    rows = seq // GRID_W
    return x.reshape(b, rows, GRID_W, ch).transpose(0, 2, 1, 3).reshape(b, seq, ch)


def _to_row_major(x):
    b, seq, ch = x.shape
    rows = seq // GRID_W
    return x.reshape(b, GRID_W, rows, ch).transpose(0, 2, 1, 3).reshape(b, seq, ch)


def kernel(x_prompt, x_sample, state_l0, state_l2, c, c_ctx, ada_w, ada_b, norm_w, dn_w_in, dn_conv_w,
           dn_conv_b, dn_a_log, dn_dt_bias, dn_norm_w, dn_w_out, cf_w_in, cf_conv_w, cf_conv_b, cf_ln_w,
           cf_ln_b, cf_w_out, final_norm_w):
    depth = ada_w.shape[0]
    bs = x_sample.shape[0]
    d = x_prompt.shape[-1]
    rows = -(-(1 + bs) // SUBLANES) * SUBLANES
    cond = jnp.zeros((rows, d), F32).at[0].set(c_ctx).at[1:1 + bs].set(c)
    mods = _ada_mods(cond, ada_w, ada_b)
    caches = (state_l0, state_l2)
    new_states = []
    xp, xs = x_prompt, x_sample
    for i in range(depth):
        j = i // 2
        mp = mods[i, 0:1][:, None, :]
        ms = mods[i, 1:1 + bs][:, None, :]
        if i % 2 == 0:
            dn = (j, dn_w_in, dn_conv_w[j], dn_conv_b[j], dn_a_log[j], dn_dt_bias[j], dn_norm_w[j], dn_w_out)
            xp, sp = _dn_layer(xp, mp, norm_w[i], *dn, s0=None, nb_conv=min(8, xp.shape[0]))
            new_states.append(sp)
            if j % 2 == 1:
                xs_cm, _ = _dn_layer(_to_col_major(xs), ms, norm_w[i], *dn, s0=caches[j], nb_conv=1)
                xs = _to_row_major(xs_cm)
            else:
                xs, _ = _dn_layer(xs, ms, norm_w[i], *dn, s0=caches[j], nb_conv=1)
        else:
            cf = (j, cf_w_in, cf_conv_w[j], cf_conv_b[j], cf_ln_w[j], cf_ln_b[j], cf_w_out)
            xp = _cf_layer(xp, mp, norm_w[i], *cf, mode="seq")
            xs = _cf_layer(xs, ms, norm_w[i], *cf, mode="row" if j % 2 == 0 else "col")
    bp, sp_len, _ = xp.shape
    y_prompt = _final_norm(xp.reshape(bp * sp_len, d), final_norm_w).reshape(xp.shape)
    y_sample = _final_norm(xs.reshape(bs * xs.shape[1], d), final_norm_w).reshape(xs.shape)
    return (y_prompt, y_sample, new_states[0], new_states[1])
```
